```python
import jax
import jax.numpy as jnp
from jax import lax
import numpy as np

D_MODEL = 1024
BATCH = 4
SEQ = 4096
DEPTH = 2

GRID_W = 64
CTX_LEN = 256
NA_HEADS = 8
HEAD_DIM = 64
NA_WIDTH = NA_HEADS * HEAD_DIM
WIN_H = 8
WIN_W = 16
Q_COLS = 16
BAND_W = Q_COLS + WIN_W
ROPE_BASE = 10000.0
SG_GROUPS = 8
SG_WIDTH = 512
SG_GROUP_DIM = SG_WIDTH // SG_GROUPS
CHUNK = 128
D_FF = 2816
CONV_W = 3
IN_SIZES = (NA_WIDTH, NA_WIDTH, NA_WIDTH, SG_WIDTH, SG_WIDTH, D_MODEL, D_MODEL)
N_IN = sum(IN_SIZES)
ALPHA = (2 * DEPTH) ** 0.25
BETA = (8 * DEPTH) ** -0.25
LN_EPS = 1e-5
NEG_INF = -1e30

kernel_name = "hybrid_na_sgmlp_diffusion_block"


def layer_norm(x, g, b):
    xf = x.astype(jnp.float32)
    mu = jnp.mean(xf, axis=-1, keepdims=True)
    var = jnp.mean(jnp.square(xf - mu), axis=-1, keepdims=True)
    y = (xf - mu) * lax.rsqrt(var + LN_EPS)
    return (y * g.astype(jnp.float32) + b.astype(jnp.float32)).astype(x.dtype)


def modulate(x, shift, scale):
    return x * (1 + scale) + shift


def split_in(z):
    offs = [int(o) for o in np.cumsum(IN_SIZES)[:-1]]
    return jnp.split(z, offs, axis=-1)


def heads(t):
    return t.reshape(t.shape[0], t.shape[1], NA_HEADS, HEAD_DIM)


def axial_rope(x, row, col):
    half = HEAD_DIM // 2
    nf = half // 2
    inv = ROPE_BASE ** (-jnp.arange(nf, dtype=jnp.float32) / nf)

    def rot(xa, pos):
        ang = pos.astype(jnp.float32)[:, None] * inv[None, :]
        cos = jnp.cos(ang)[None, :, None, :]
        sin = jnp.sin(ang)[None, :, None, :]
        x1 = xa[..., :nf].astype(jnp.float32)
        x2 = xa[..., nf:].astype(jnp.float32)
        return jnp.concatenate([x1 * cos - x2 * sin, x2 * cos + x1 * sin], axis=-1)

    out = jnp.concatenate([rot(x[..., :half], row), rot(x[..., half:], col)], axis=-1)
    return out.astype(x.dtype)


def na_latent(q, k, v, k_ctx, v_ctx, rpb):
    B, n, H, hd = q.shape
    rows = n // GRID_W
    kh = min(WIN_H, rows)
    scale = hd ** -0.5
    t = jnp.arange(n)
    q_rot = axial_rope(q, t // GRID_W, t % GRID_W)
    k_rot = axial_rope(k, t // GRID_W, t % GRID_W)

    r_idx = np.arange(rows)
    row_start = np.clip(r_idx - WIN_H // 2, 0, rows - kh)
    key_rows = row_start[:, None] + np.arange(kh)
    n_qb = GRID_W // Q_COLS
    jb = np.arange(n_qb)
    band_start = np.clip(jb * Q_COLS - WIN_W // 2, 0, GRID_W - BAND_W)
    band_cols = band_start[:, None] + np.arange(BAND_W)
    q_cols = jb[:, None] * Q_COLS + np.arange(Q_COLS)
    win_start = np.clip(q_cols - WIN_W // 2, 0, GRID_W - WIN_W)
    kc = band_cols[:, None, :]
    col_ok = (kc >= win_start[..., None]) & (kc < win_start[..., None] + WIN_W)
    dx_idx = np.clip(kc - q_cols[..., None] + WIN_W - 1, 0, 2 * WIN_W - 2)
    dy_idx = key_rows - r_idx[:, None] + WIN_H - 1
    tok_idx = key_rows[:, None, :, None] * GRID_W + band_cols[None, :, None, :]
    col_ok_j = jnp.asarray(col_ok)[:, :, None, :]
    dx_j = jnp.asarray(dx_idx)
    n_lat = kh * BAND_W

    q_rows = q_rot.reshape(B, rows, n_qb, Q_COLS, H, hd).transpose(1, 0, 2, 3, 4, 5)
    qp_rows = q.reshape(B, rows, n_qb, Q_COLS, H, hd).transpose(1, 0, 2, 3, 4, 5)

    def one_row(args):
        q_r, qp_r, tok_r, dy_r = args
        k_b = k_rot[:, tok_r]
        v_b = v[:, tok_r].reshape(B, n_qb, n_lat, H, hd)
        s_lat = jnp.einsum('bjqhd,bjyxhd->bhjqyx', q_r, k_b).astype(jnp.float32) * scale
        bias = rpb[:, dy_r][:, :, dx_j].transpose(0, 2, 3, 1, 4)
        s_lat = jnp.where(col_ok_j, s_lat + bias.astype(jnp.float32), NEG_INF)
        s_lat = s_lat.reshape(B, H, n_qb, Q_COLS, n_lat)
        s_ctx = jnp.einsum('bjqhd,bchd->bhjqc', qp_r, k_ctx).astype(jnp.float32) * scale
        p = jax.nn.softmax(jnp.concatenate([s_lat, s_ctx], axis=-1), axis=-1)
        p_lat, p_ctx = p[..., :n_lat], p[..., n_lat:]
        o = (jnp.einsum('bhjqk,bjkhd->bjqhd', p_lat, v_b.astype(jnp.float32))
             + jnp.einsum('bhjqc,bchd->bjqhd', p_ctx, v_ctx.astype(jnp.float32)))
        return o.astype(q.dtype)

    out = lax.map(one_row, (q_rows, qp_rows, jnp.asarray(tok_idx, dtype=jnp.int32),
                            jnp.asarray(dy_idx, dtype=jnp.int32)))
    return out.transpose(1, 0, 2, 3, 4, 5).reshape(B, n, H * hd)


def na_context(q, k, v):
    B, n, H, hd = q.shape
    s = jnp.einsum('bqhd,bkhd->bhqk', q, k).astype(jnp.float32) * (hd ** -0.5)
    p = jax.nn.softmax(s, axis=-1)
    o = jnp.einsum('bhqk,bkhd->bqhd', p, v.astype(jnp.float32))
    return o.astype(q.dtype).reshape(B, n, H * hd)


def spatial_gating(u, v, ln_g, ln_b, w_s, b_s):
    B, n, _ = u.shape
    v = layer_norm(v, ln_g, ln_b)
    vc = v.reshape(B, n // CHUNK, CHUNK, SG_GROUPS, SG_GROUP_DIM)
    mixed = jnp.einsum('gpq,bcqgd->bcpgd', w_s, vc) + b_s.T[None, None, :, :, None]
    return u * mixed.reshape(B, n, SG_WIDTH)


def branch_merge(o_a, u, sv, ga, gb, sg_ln_g, sg_ln_b, w_s, b_s, w_pa, w_pb, w_o):
    o_b = spatial_gating(jax.nn.gelu(u), jax.nn.gelu(sv), sg_ln_g, sg_ln_b, w_s, b_s)
    y = jax.nn.sigmoid(ga) * (o_a @ w_pa) + jax.nn.sigmoid(gb) * (o_b @ w_pb)
    return y @ w_o


def conv_ffn(h, w_up, conv_w, conv_b, w_down):
    z = h @ w_up
    z = lax.conv_general_dilated(z, conv_w[:, None, :], window_strides=(1,),
                                 padding=((CONV_W // 2, CONV_W // 2),),
                                 dimension_numbers=('NWC', 'WIO', 'NWC'),
                                 feature_group_count=2 * D_FF) + conv_b
    a, g = jnp.split(z, 2, axis=-1)
    return (jax.nn.silu(g) * a) @ w_down


def setup_inputs(seed: int = 0) -> dict:
    key = jax.random.key(seed)
    ks = jax.random.split(key, 24)
    f32 = jnp.float32

    def nrm(k, shape, s):
        return jax.random.normal(k, shape, f32) * s

    col_scale = jnp.concatenate([jnp.ones((2 * NA_WIDTH,), f32), jnp.full((NA_WIDTH,), BETA, f32),
                                 jnp.ones((N_IN - 3 * NA_WIDTH,), f32)])
    return {
        "x": nrm(ks[0], (BATCH, SEQ, D_MODEL), 1.0),
        "c": nrm(ks[1], (BATCH, D_MODEL), 1.0),
        "ctx": nrm(ks[2], (BATCH, CTX_LEN, D_MODEL), 1.0),
        "c_ctx": nrm(ks[3], (D_MODEL,), 1.0),
        "w_ada": nrm(ks[4], (DEPTH, D_MODEL, 6 * D_MODEL), 0.5 * D_MODEL ** -0.5),
        "b_ada": nrm(ks[5], (DEPTH, 6 * D_MODEL), 0.01),
        "w_in": nrm(ks[6], (DEPTH, D_MODEL, N_IN), D_MODEL ** -0.5) * col_scale,
        "rpb": nrm(ks[7], (DEPTH, NA_HEADS, 2 * WIN_H - 1, 2 * WIN_W - 1), 0.1),
        "sg_ln_g": 1.0 + nrm(ks[8], (DEPTH, SG_WIDTH), 0.01),
        "sg_ln_b": nrm(ks[9], (DEPTH, SG_WIDTH), 0.01),
        "w_s": nrm(ks[10], (DEPTH, SG_GROUPS, CHUNK, CHUNK), CHUNK ** -0.5),
        "b_s": 1.0 + nrm(ks[11], (DEPTH, SG_GROUPS, CHUNK), 0.01),
        "w_pa": nrm(ks[12], (DEPTH, NA_WIDTH, D_MODEL), BETA * NA_WIDTH ** -0.5),
        "w_pb": nrm(ks[13], (DEPTH, SG_WIDTH, D_MODEL), BETA * SG_WIDTH ** -0.5),
        "w_o": nrm(ks[14], (DEPTH, D_MODEL, D_MODEL), BETA * D_MODEL ** -0.5),
        "ln1_g": 1.0 + nrm(ks[15], (DEPTH, D_MODEL), 0.01),
        "ln1_b": nrm(ks[16], (DEPTH, D_MODEL), 0.01),
        "w_up": nrm(ks[17], (DEPTH, D_MODEL, 2 * D_FF), D_MODEL ** -0.5),
        "conv_w": nrm(ks[18], (DEPTH, CONV_W, 2 * D_FF), CONV_W ** -0.5),
        "conv_b": nrm(ks[19], (DEPTH, 2 * D_FF), 0.01),
        "w_down": nrm(ks[20], (DEPTH, D_FF, D_MODEL), BETA * D_FF ** -0.5),
        "ln2_g": 1.0 + nrm(ks[21], (DEPTH, D_MODEL), 0.01),
        "ln2_b": nrm(ks[22], (DEPTH, D_MODEL), 0.01),
    }


def reference(x, c, ctx, c_ctx, w_ada, b_ada, w_in, rpb, sg_ln_g, sg_ln_b, w_s, b_s,
              w_pa, w_pb, w_o, ln1_g, ln1_b, w_up, conv_w, conv_b, w_down, ln2_g, ln2_b):
    for i in range(DEPTH):
        mod = jax.nn.silu(c) @ w_ada[i] + b_ada[i]
        mod_c = jax.nn.silu(c_ctx) @ w_ada[i] + b_ada[i]
        sh_a, sc_a, g_a, sh_f, sc_f, g_f = [m[:, None, :] for m in jnp.split(mod, 6, axis=-1)]
        csh_a, csc_a, cg_a, csh_f, csc_f, cg_f = jnp.split(mod_c, 6, axis=-1)

        hc = modulate(ctx, csh_a, csc_a)
        if i < DEPTH - 1:
            qc, kc, vc, uc, svc, gac, gbc = split_in(hc @ w_in[i])
            k_c, v_c = heads(kc), heads(vc)
            oa_c = na_context(heads(qc), k_c, v_c)
            y_c = branch_merge(oa_c, uc, svc, gac, gbc, sg_ln_g[i], sg_ln_b[i], w_s[i], b_s[i],
                               w_pa[i], w_pb[i], w_o[i])
            ctx_next = layer_norm(ALPHA * ctx + cg_a * y_c, ln1_g[i], ln1_b[i])
            f_c = conv_ffn(modulate(ctx_next, csh_f, csc_f), w_up[i], conv_w[i], conv_b[i], w_down[i])
            ctx_next = layer_norm(ALPHA * ctx_next + cg_f * f_c, ln2_g[i], ln2_b[i])
        else:
            kc, vc = jnp.split(hc @ w_in[i][:, NA_WIDTH:3 * NA_WIDTH], 2, axis=-1)
            k_c, v_c = heads(kc), heads(vc)
            ctx_next = ctx

        h = modulate(x, sh_a, sc_a)
        q, k, v, u, sv, ga, gb = split_in(h @ w_in[i])
        oa = na_latent(heads(q), heads(k), heads(v), k_c, v_c, rpb[i])
        y = branch_merge(oa, u, sv, ga, gb, sg_ln_g[i], sg_ln_b[i], w_s[i], b_s[i],
                         w_pa[i], w_pb[i], w_o[i])
        x = layer_norm(ALPHA * x + g_a * y, ln1_g[i], ln1_b[i])
        f = conv_ffn(modulate(x, sh_f, sc_f), w_up[i], conv_w[i], conv_b[i], w_down[i])
        x = layer_norm(ALPHA * x + g_f * f, ln2_g[i], ln2_b[i])
        ctx = ctx_next
    return x
```

```python
import functools

import numpy as np
import jax
import jax.numpy as jnp
from jax import lax
from jax.experimental import pallas as pl
from jax.experimental.pallas import tpu as pltpu

D_MODEL = 1024
BATCH = 4
SEQ = 4096
DEPTH = 2
GRID_W = 64
CTX_LEN = 256
NA_HEADS = 8
HEAD_DIM = 64
NA_WIDTH = NA_HEADS * HEAD_DIM
WIN_H = 8
WIN_W = 16
ROPE_BASE = 10000.0
SG_GROUPS = 8
SG_WIDTH = 512
CHUNK = 128
D_FF = 2816
N_IN = 3 * NA_WIDTH + 2 * SG_WIDTH + 2 * D_MODEL
ALPHA = (2 * DEPTH) ** 0.25
LN_EPS = 1e-5
NEG_INF = -1e30

F32 = jnp.float32
BF16 = jnp.bfloat16

LANES = 128
HEAD_PAIRS = NA_HEADS // 2
MOD_ROWS = 8
CTX_MOD_ROW = BATCH
Q_ROWS = 4
Q_BLOCK = Q_ROWS * GRID_W
K_ROWS = 3 * Q_ROWS
FF_CHUNK = 256
N_FF_CHUNKS = D_FF // FF_CHUNK
HALO = 8
VMEM_LIMIT = 56 * 1024 * 1024


def _dot(a, b):
    return jnp.dot(a, b, preferred_element_type=F32)


def _dot_nt(a, b):
    return lax.dot_general(a, b, (((1,), (1,)), ((), ())), preferred_element_type=F32)


def _layer_norm(v, g, b):
    mu = jnp.mean(v, axis=-1, keepdims=True)
    d = v - mu
    var = jnp.mean(d * d, axis=-1, keepdims=True)
    return d * lax.rsqrt(var + LN_EPS) * g + b


def _params(n_axes):
    return pltpu.CompilerParams(dimension_semantics=("arbitrary",) * n_axes,
                                vmem_limit_bytes=VMEM_LIMIT)


def _resident(shape):
    zeros = (0,) * len(shape)
    return pl.BlockSpec(shape, lambda *_: zeros, pipeline_mode=pl.Buffered(1))


def _mod_kernel(cc_ref, w_ref, b_ref, o_ref):
    s = jax.nn.silu(cc_ref[...]).astype(BF16)
    o_ref[...] = _dot(s, w_ref[...].astype(BF16)) + b_ref[...]


def _modulation(cc, w_ada, b_ada):
    tn = 1536
    n6 = 6 * D_MODEL
    out = pl.pallas_call(
        _mod_kernel,
        grid=(DEPTH, n6 // tn),
        in_specs=[pl.BlockSpec((MOD_ROWS, D_MODEL), lambda l, n: (0, 0)),
                  pl.BlockSpec((None, D_MODEL, tn), lambda l, n: (l, 0, n)),
                  pl.BlockSpec((None, 1, tn), lambda l, n: (l, 0, n))],
        out_specs=pl.BlockSpec((None, MOD_ROWS, tn), lambda l, n: (l, 0, n)),
        out_shape=jax.ShapeDtypeStruct((DEPTH, MOD_ROWS, n6), F32),
        compiler_params=_params(2),
        name="adaln_modulation",
    )(cc, w_ada, b_ada.reshape(DEPTH, 1, n6))
    return out.reshape(DEPTH, MOD_ROWS, 6, D_MODEL)


def _rope_store(z, cos, sin, first_half, out_ref, scale):
    for j in range(NA_WIDTH // LANES):
        zj = z[:, j * LANES:(j + 1) * LANES]
        partner = jnp.where(first_half, pltpu.roll(zj, LANES - 16, 1), pltpu.roll(zj, 16, 1))
        r = zj * cos + partner * sin
        if scale != 1.0:
            r = r * scale
        out_ref[:, j * LANES:(j + 1) * LANES] = r.astype(out_ref.dtype)


def _inproj_kernel(*refs, mode):
    if mode == "latent":
        (x_ref, mod_ref, w_ref, lng_ref, lnb_ref, cos_ref, sin_ref,
         qr_ref, qp_ref, kr_ref, v_ref, gu_ref, svn_ref, sga_ref, sgb_ref) = refs
    elif mode == "ctx":
        (x_ref, mod_ref, w_ref, lng_ref, lnb_ref,
         qp_ref, kr_ref, v_ref, gu_ref, svn_ref, sga_ref, sgb_ref) = refs
    else:
        x_ref, mod_ref, w_ref, kr_ref, v_ref = refs

    scale = HEAD_DIM ** -0.5
    h = (x_ref[...] * (1.0 + mod_ref[1:2, :]) + mod_ref[0:1, :]).astype(BF16)

    def proj(lo, width):
        return _dot(h, w_ref[:, lo:lo + width])

    if mode == "ctx_kv":
        kr_ref[...] = proj(0, NA_WIDTH).astype(BF16)
        v_ref[...] = proj(NA_WIDTH, NA_WIDTH).astype(BF16)
        return

    zq = proj(0, NA_WIDTH)
    zk = proj(NA_WIDTH, NA_WIDTH)
    qp_ref[...] = (zq * scale).astype(BF16)
    if mode == "latent":
        cos = cos_ref[...]
        sin = sin_ref[...]
        lane = lax.broadcasted_iota(jnp.int32, cos.shape, 1)
        first_half = (lane % 32) < 16
        _rope_store(zq, cos, sin, first_half, qr_ref, scale)
        _rope_store(zk, cos, sin, first_half, kr_ref, 1.0)
    else:
        kr_ref[...] = zk.astype(BF16)
    v_ref[...] = proj(2 * NA_WIDTH, NA_WIDTH).astype(BF16)
    gu_ref[...] = jax.nn.gelu(proj(3 * NA_WIDTH, SG_WIDTH)).astype(BF16)
    sv = jax.nn.gelu(proj(3 * NA_WIDTH + SG_WIDTH, SG_WIDTH))
    svn_ref[...] = _layer_norm(sv, lng_ref[...], lnb_ref[...]).astype(BF16)
    lo = 3 * NA_WIDTH + 2 * SG_WIDTH
    sga_ref[...] = jax.nn.sigmoid(proj(lo, D_MODEL)).astype(BF16)
    sgb_ref[...] = jax.nn.sigmoid(proj(lo + D_MODEL, D_MODEL)).astype(BF16)


def _in_projection(x2d, mod, w, lng, lnb, rope, *, mode, tm, seq):
    t = x2d.shape[0]
    tiles_per_seq = seq // tm
    if mode == "latent":
        mod_map = lambda i: (i // tiles_per_seq, 0, 0)
    else:
        mod_map = lambda i: (CTX_MOD_ROW, 0, 0)
    tok = lambda width: pl.BlockSpec((tm, width), lambda i: (i, 0))
    in_specs = [tok(D_MODEL), pl.BlockSpec((None, 6, D_MODEL), mod_map), _resident(w.shape)]
    args = [x2d, mod, w]
    if mode != "ctx_kv":
        in_specs += [_resident((1, SG_WIDTH)), _resident((1, SG_WIDTH))]
        args += [lng, lnb]
    if mode == "latent":
        pos = pl.BlockSpec((tm, LANES), lambda i: (i % tiles_per_seq, 0))
        in_specs += [pos, pos]
        args += list(rope)
    widths = {"latent": [NA_WIDTH] * 6 + [D_MODEL] * 2,
              "ctx": [NA_WIDTH] * 5 + [D_MODEL] * 2,
              "ctx_kv": [NA_WIDTH] * 2}[mode]
    return pl.pallas_call(
        functools.partial(_inproj_kernel, mode=mode),
        grid=(t // tm,),
        in_specs=in_specs,
        out_specs=[tok(wd) for wd in widths],
        out_shape=[jax.ShapeDtypeStruct((t, wd), BF16) for wd in widths],
        compiler_params=_params(1),
        name="in_projection_" + mode,
    )(*args)


def _mix_kernel(*refs, latent):
    if latent:
        (x_ref, mod_ref, qr_ref, qp_ref, k0_ref, k1_ref, k2_ref, v0_ref, v1_ref, v2_ref,
         kc_ref, vc_ref, tab_ref, gu_ref, svn_ref, sga_ref, sgb_ref, ws_ref, bs_ref,
         wpa_ref, wpb_ref, wo_ref, lng_ref, lnb_ref, o_ref, oa_ref, ob_ref) = refs
        k_refs = (k0_ref, k1_ref, k2_ref)
        v_refs = (v0_ref, v1_ref, v2_ref)
    else:
        (x_ref, mod_ref, qp_ref, kc_ref, vc_ref, gu_ref, svn_ref, sga_ref, sgb_ref,
         ws_ref, bs_ref, wpa_ref, wpb_ref, wo_ref, lng_ref, lnb_ref,
         o_ref, oa_ref, ob_ref) = refs
        k_refs = v_refs = ()

    lane = lax.broadcasted_iota(jnp.int32, (Q_BLOCK, LANES), 1)
    low_half = lane < HEAD_DIM

    for p in range(HEAD_PAIRS):
        cols = slice(p * LANES, (p + 1) * LANES)
        q_ctx = qp_ref[:, cols]
        kc = kc_ref[:, cols]
        vc = vc_ref[:, cols]
        outs = []
        for e in range(2):
            mine = low_half if e == 0 else jnp.logical_not(low_half)
            s_parts = [_dot_nt(jnp.where(mine, q_ctx, 0), kc)]
            if latent:
                q_lat = jnp.where(mine, qr_ref[:, cols], 0)
                for m in range(3):
                    s = _dot_nt(q_lat, k_refs[m][:, cols])
                    s_parts.append(s + tab_ref[2 * p + e, :, m * Q_BLOCK:(m + 1) * Q_BLOCK])
            mx = s_parts[0].max(axis=-1, keepdims=True)
            for s in s_parts[1:]:
                mx = jnp.maximum(mx, s.max(axis=-1, keepdims=True))
            p_parts = [jnp.exp(s - mx) for s in s_parts]
            denom = p_parts[0].sum(axis=-1, keepdims=True)
            for pp in p_parts[1:]:
                denom = denom + pp.sum(axis=-1, keepdims=True)
            acc = _dot(p_parts[0].astype(BF16), vc)
            for m in range(len(p_parts) - 1):
                acc = acc + _dot(p_parts[m + 1].astype(BF16), v_refs[m][:, cols])
            outs.append(acc / denom)
        oa_ref[:, cols] = jnp.where(low_half, outs[0], outs[1]).astype(BF16)

    half = lax.broadcasted_iota(jnp.int32, (CHUNK, LANES), 1) < (SG_WIDTH // SG_GROUPS)
    for c in range(Q_BLOCK // CHUNK):
        rows = slice(c * CHUNK, (c + 1) * CHUNK)
        for p in range(SG_GROUPS // 2):
            cols = slice(p * LANES, (p + 1) * LANES)
            vl = svn_ref[rows, cols]
            mixed = jnp.where(half, _dot(ws_ref[2 * p], vl), _dot(ws_ref[2 * p + 1], vl))
            mixed = mixed + bs_ref[p]
            ob_ref[rows, cols] = (gu_ref[rows, cols].astype(F32) * mixed).astype(BF16)

    ya = _dot(oa_ref[...], wpa_ref[...])
    yb = _dot(ob_ref[...], wpb_ref[...])
    y = sga_ref[...].astype(F32) * ya + sgb_ref[...].astype(F32) * yb
    out = _dot(y.astype(BF16), wo_ref[...])
    r = ALPHA * x_ref[...] + mod_ref[2:3, :] * out
    o_ref[...] = _layer_norm(r, lng_ref[...], lnb_ref[...])


def _mix(x2d, mod, acts, ctx_kv, tab, wts, *, latent):
    t = x2d.shape[0]
    blocks_per_seq = (SEQ if latent else CTX_LEN) // Q_BLOCK
    tok = lambda width: pl.BlockSpec((Q_BLOCK, width), lambda b, j: (b * blocks_per_seq + j, 0))
    ctx_spec = pl.BlockSpec((CTX_LEN, NA_WIDTH), lambda b, j: (b, 0))
    ws, bs, wpa, wpb, wo, lng, lnb = wts
    w_specs = [_resident(a.shape) for a in wts]
    if latent:
        qr, qp, kr, v, gu, svn, sga, sgb = acts
        kc, vc = ctx_kv
        mod_map = lambda b, j: (b, 0, 0)

        def nbr(m):
            return pl.BlockSpec(
                (Q_BLOCK, NA_WIDTH),
                lambda b, j: (b * blocks_per_seq + jnp.clip(j - 1 + m, 0, blocks_per_seq - 1), 0))

        last = blocks_per_seq - 1
        tab_spec = pl.BlockSpec(
            (None, NA_HEADS, Q_BLOCK, K_ROWS * GRID_W),
            lambda b, j: (jnp.where(j == 0, 0, jnp.where(j == last, 2, 1)), 0, 0, 0))
        in_specs = ([tok(D_MODEL), pl.BlockSpec((None, 6, D_MODEL), mod_map),
                     tok(NA_WIDTH), tok(NA_WIDTH)]
                    + [nbr(m) for m in range(3)] + [nbr(m) for m in range(3)]
                    + [ctx_spec, ctx_spec, tab_spec]
                    + [tok(NA_WIDTH), tok(NA_WIDTH), tok(D_MODEL), tok(D_MODEL)] + w_specs)
        args = [x2d, mod, qr, qp, kr, kr, kr, v, v, v, kc, vc, tab, gu, svn, sga, sgb] + list(wts)
    else:
        qp, kc, vc, gu, svn, sga, sgb = acts
        mod_map = lambda b, j: (CTX_MOD_ROW, 0, 0)
        in_specs = ([tok(D_MODEL), pl.BlockSpec((None, 6, D_MODEL), mod_map),
                     tok(NA_WIDTH), ctx_spec, ctx_spec]
                    + [tok(NA_WIDTH), tok(NA_WIDTH), tok(D_MODEL), tok(D_MODEL)] + w_specs)
        args = [x2d, mod, qp, kc, vc, gu, svn, sga, sgb] + list(wts)
    return pl.pallas_call(
        functools.partial(_mix_kernel, latent=latent),
        grid=(BATCH, blocks_per_seq),
        in_specs=in_specs,
        out_specs=tok(D_MODEL),
        out_shape=jax.ShapeDtypeStruct((t, D_MODEL), F32),
        scratch_shapes=[pltpu.VMEM((Q_BLOCK, NA_WIDTH), BF16),
                        pltpu.VMEM((Q_BLOCK, SG_WIDTH), BF16)],
        compiler_params=_params(2),
        name="mix_latent" if latent else "mix_ctx",
    )(*args)


def _ffn_kernel(x_ref, prev_ref, next_ref, mod_ref, wa_ref, wg_ref, cp_ref, wd_ref,
                lng_ref, lnb_ref, o_ref, za_ref, zg_ref, acc_ref, *, tm, tiles_per_seq):
    i = pl.program_id(0)
    has_prev = (i % tiles_per_seq != 0).astype(F32)
    has_next = (i % tiles_per_seq != tiles_per_seq - 1).astype(F32)
    shift = mod_ref[3:4, :]
    scale1 = 1.0 + mod_ref[4:5, :]
    x = x_ref[...]
    h = (x * scale1 + shift).astype(BF16)
    edge = jnp.concatenate([(prev_ref[...] * scale1 + shift) * has_prev,
                            (next_ref[...] * scale1 + shift) * has_next], axis=0).astype(BF16)
    acc_ref[...] = jnp.zeros_like(acc_ref)

    def conv(z_ref, w_ref, cp, row):
        z_ref[HALO:HALO + tm, :] = _dot(h, w_ref)
        ze = _dot(edge, w_ref)
        z_ref[0:HALO, :] = ze[0:HALO]
        z_ref[HALO + tm:2 * HALO + tm, :] = ze[HALO:2 * HALO]
        return (cp[row:row + 1] * z_ref[HALO - 1:HALO - 1 + tm, :]
                + cp[row + 1:row + 2] * z_ref[HALO:HALO + tm, :]
                + cp[row + 2:row + 3] * z_ref[HALO + 1:HALO + 1 + tm, :]
                + cp[row + 3:row + 4])

    def body(c, carry):
        cp = cp_ref[c]
        a = conv(za_ref, wa_ref[c], cp, 0)
        g = conv(zg_ref, wg_ref[c], cp, 4)
        act = (jax.nn.silu(g) * a).astype(BF16)
        acc_ref[...] += _dot(act, wd_ref[c])
        return carry

    lax.fori_loop(0, N_FF_CHUNKS, body, 0)
    r = ALPHA * x + mod_ref[5:6, :] * acc_ref[...]
    o_ref[...] = _layer_norm(r, lng_ref[...], lnb_ref[...])


def _conv_ffn(x2d, mod, wts, *, latent, tm, seq):
    t = x2d.shape[0]
    tiles_per_seq = seq // tm
    n_halo_blocks = t // HALO
    per_tile = tm // HALO
    if latent:
        mod_map = lambda i: (i // tiles_per_seq, 0, 0)
    else:
        mod_map = lambda i: (CTX_MOD_ROW, 0, 0)
    tok = pl.BlockSpec((tm, D_MODEL), lambda i: (i, 0))
    prev = pl.BlockSpec((HALO, D_MODEL), lambda i: (jnp.maximum(i * per_tile - 1, 0), 0))
    nxt = pl.BlockSpec((HALO, D_MODEL),
                       lambda i: (jnp.minimum((i + 1) * per_tile, n_halo_blocks - 1), 0))
    return pl.pallas_call(
        functools.partial(_ffn_kernel, tm=tm, tiles_per_seq=tiles_per_seq),
        grid=(t // tm,),
        in_specs=[tok, prev, nxt, pl.BlockSpec((None, 6, D_MODEL), mod_map)]
                 + [_resident(a.shape) for a in wts],
        out_specs=tok,
        out_shape=jax.ShapeDtypeStruct((t, D_MODEL), F32),
        scratch_shapes=[pltpu.VMEM((tm + 2 * HALO, FF_CHUNK), F32),
                        pltpu.VMEM((tm + 2 * HALO, FF_CHUNK), F32),
                        pltpu.VMEM((tm, D_MODEL), F32)],
        compiler_params=_params(1),
        name="conv_ffn_latent" if latent else "conv_ffn_ctx",
    )(x2d, x2d, x2d, mod, *wts)


def _rope_tables():
    half = HEAD_DIM // 2
    nf = half // 2
    inv = ROPE_BASE ** (-jnp.arange(nf, dtype=F32) / nf)
    t = jnp.arange(SEQ)

    def part(pos):
        ang = pos.astype(F32)[:, None] * inv[None, :]
        cos = jnp.cos(ang)
        sin = jnp.sin(ang)
        return jnp.concatenate([cos, cos], axis=-1), jnp.concatenate([-sin, sin], axis=-1)

    cr, sr = part(t // GRID_W)
    cc, sc = part(t % GRID_W)
    cos = jnp.concatenate([cr, cc], axis=-1)
    sin = jnp.concatenate([sr, sc], axis=-1)
    reps = LANES // HEAD_DIM
    return jnp.tile(cos, (1, reps)), jnp.tile(sin, (1, reps))


def _bias_tables(rpb):
    qi = np.arange(Q_ROWS)[:, None]
    ki = np.arange(K_ROWS)[None, :]
    dy = ki - qi + WIN_H // 2 - 1
    row_ok = np.stack([(ki >= Q_ROWS) & (qi >= 0),
                       (ki >= qi) & (ki < qi + WIN_H),
                       (ki < WIN_H) & (qi >= 0)])
    qc = np.arange(GRID_W)[:, None]
    kc = np.arange(GRID_W)[None, :]
    win_start = np.clip(qc - WIN_W // 2, 0, GRID_W - WIN_W)
    col_ok = (kc >= win_start) & (kc < win_start + WIN_W)
    dx = np.clip(kc - qc + WIN_W - 1, 0, 2 * WIN_W - 2)
    dy = np.clip(dy, 0, 2 * WIN_H - 2)
    bias = rpb[:, dy[:, None, :, None], dx[None, :, None, :]]
    ok = row_ok[:, :, None, :, None] & col_ok[None, None, :, None, :]
    tab = jnp.where(jnp.asarray(ok)[:, None], bias[None].astype(F32), NEG_INF)
    return tab.reshape(3, NA_HEADS, Q_BLOCK, K_ROWS * GRID_W)


def _chunk_cols(w):
    return w.reshape(w.shape[0], N_FF_CHUNKS, FF_CHUNK).transpose(1, 0, 2)


def _layer_weights(i, w_in, sg_ln_g, sg_ln_b, w_s, b_s, w_pa, w_pb, w_o, ln1_g, ln1_b,
                   w_up, conv_w, conv_b, w_down, ln2_g, ln2_b):
    row = lambda a: a.reshape(1, -1)
    bs = jnp.repeat(b_s[i].reshape(SG_GROUPS // 2, 2, CHUNK).transpose(0, 2, 1),
                    SG_WIDTH // SG_GROUPS, axis=-1)
    mix = (w_s[i].astype(BF16), bs, w_pa[i].astype(BF16), w_pb[i].astype(BF16),
           w_o[i].astype(BF16), row(ln1_g[i]), row(ln1_b[i]))
    cw, cb = conv_w[i], conv_b[i]
    cp = jnp.concatenate([cw[:, :D_FF], cb[None, :D_FF], cw[:, D_FF:], cb[None, D_FF:]], axis=0)
    cp = cp.reshape(8, N_FF_CHUNKS, FF_CHUNK).transpose(1, 0, 2)
    ffn = (_chunk_cols(w_up[i][:, :D_FF]).astype(BF16), _chunk_cols(w_up[i][:, D_FF:]).astype(BF16),
           cp, w_down[i].astype(BF16).reshape(N_FF_CHUNKS, FF_CHUNK, D_MODEL),
           row(ln2_g[i]), row(ln2_b[i]))
    return w_in[i].astype(BF16), row(sg_ln_g[i]), row(sg_ln_b[i]), mix, ffn


def kernel(x, c, ctx, c_ctx, w_ada, b_ada, w_in, rpb, sg_ln_g, sg_ln_b, w_s, b_s, w_pa, w_pb, w_o,
           ln1_g, ln1_b, w_up, conv_w, conv_b, w_down, ln2_g, ln2_b):
    assert x.shape == (BATCH, SEQ, D_MODEL) and ctx.shape == (BATCH, CTX_LEN, D_MODEL)
    cc = jnp.concatenate([c, c_ctx[None, :], jnp.zeros((MOD_ROWS - BATCH - 1, D_MODEL), F32)], axis=0)
    mods = _modulation(cc, w_ada, b_ada)
    rope = _rope_tables()
    xl = x.reshape(BATCH * SEQ, D_MODEL)
    xc = ctx.reshape(BATCH * CTX_LEN, D_MODEL)
    for i in range(DEPTH):
        w_in_i, lng, lnb, mix_w, ffn_w = _layer_weights(
            i, w_in, sg_ln_g, sg_ln_b, w_s, b_s, w_pa, w_pb, w_o, ln1_g, ln1_b,
            w_up, conv_w, conv_b, w_down, ln2_g, ln2_b)
        mod = mods[i]
        tab = _bias_tables(rpb[i])
        if i < DEPTH - 1:
            c_acts = _in_projection(xc, mod, w_in_i, lng, lnb, None, mode="ctx", tm=CTX_LEN, seq=CTX_LEN)
            ctx_kv = (c_acts[1], c_acts[2])
        else:
            ctx_kv = _in_projection(xc, mod, w_in_i[:, NA_WIDTH:3 * NA_WIDTH], None, None, None,
                                    mode="ctx_kv", tm=CTX_LEN, seq=CTX_LEN)
        acts = _in_projection(xl, mod, w_in_i, lng, lnb, rope, mode="latent", tm=512, seq=SEQ)
        xl = _mix(xl, mod, acts, ctx_kv, tab, mix_w, latent=True)
        xl = _conv_ffn(xl, mod, ffn_w, latent=True, tm=512, seq=SEQ)
        if i < DEPTH - 1:
            xc = _mix(xc, mod, c_acts, None, None, mix_w, latent=False)
            xc = _conv_ffn(xc, mod, ffn_w, latent=False, tm=CTX_LEN, seq=CTX_LEN)
    return xl.reshape(BATCH, SEQ, D_MODEL)
```

```python
import functools

import jax
import jax.numpy as jnp
from jax import lax
from jax.experimental import pallas as pl
from jax.experimental.pallas import tpu as pltpu

D_MODEL = 1024
BATCH = 4
SEQ = 4096
DEPTH = 2
GRID_W = 64
CTX_LEN = 256
NA_HEADS = 8
HEAD_DIM = 64
NA_WIDTH = NA_HEADS * HEAD_DIM
WIN_H = 8
WIN_W = 16
ROPE_BASE = 10000.0
SG_GROUPS = 8
SG_WIDTH = 512
CHUNK = 128
D_FF = 2816
N_IN = 3 * NA_WIDTH + 2 * SG_WIDTH + 2 * D_MODEL
ALPHA = (2 * DEPTH) ** 0.25
LN_EPS = 1e-5
NEG_INF = -1e30

F32 = jnp.float32
BF16 = jnp.bfloat16

LANES = 128
HEAD_PAIRS = NA_HEADS // 2
MOD_ROWS = 8
CTX_MOD_ROW = BATCH
Q_ROWS = 4
Q_BLOCK = Q_ROWS * GRID_W
K_ROWS = 3 * Q_ROWS
FF_CHUNK = 256
N_FF_CHUNKS = D_FF // FF_CHUNK
HALO = 8
VMEM_LIMIT = 56 * 1024 * 1024


def _dot(a, b):
    return jnp.dot(a, b, preferred_element_type=F32)


def _dot_nt(a, b):
    return lax.dot_general(a, b, (((1,), (1,)), ((), ())), preferred_element_type=F32)


def _layer_norm(v, g, b):
    mu = jnp.mean(v, axis=-1, keepdims=True)
    d = v - mu
    var = jnp.mean(d * d, axis=-1, keepdims=True)
    return d * lax.rsqrt(var + LN_EPS) * g + b


def _params(n_axes):
    return pltpu.CompilerParams(dimension_semantics=("arbitrary",) * n_axes,
                                vmem_limit_bytes=VMEM_LIMIT)


def _resident(shape):
    zeros = (0,) * len(shape)
    return pl.BlockSpec(shape, lambda *_: zeros, pipeline_mode=pl.Buffered(1))


def _mod_kernel(cc_ref, w_ref, b_ref, o_ref):
    s = jax.nn.silu(cc_ref[...]).astype(BF16)
    o_ref[...] = _dot(s, w_ref[...].astype(BF16)) + b_ref[...]


def _modulation(cc, w_ada, b_ada):
    tn = 1536
    n6 = 6 * D_MODEL
    out = pl.pallas_call(
        _mod_kernel,
        grid=(DEPTH, n6 // tn),
        in_specs=[pl.BlockSpec((MOD_ROWS, D_MODEL), lambda l, n: (0, 0)),
                  pl.BlockSpec((None, D_MODEL, tn), lambda l, n: (l, 0, n)),
                  pl.BlockSpec((None, 1, tn), lambda l, n: (l, 0, n))],
        out_specs=pl.BlockSpec((None, MOD_ROWS, tn), lambda l, n: (l, 0, n)),
        out_shape=jax.ShapeDtypeStruct((DEPTH, MOD_ROWS, n6), F32),
        compiler_params=_params(2),
        name="adaln_modulation",
    )(cc, w_ada, b_ada.reshape(DEPTH, 1, n6))
    return out.reshape(DEPTH, MOD_ROWS, 6, D_MODEL)


def _rope_store(z, cos, sin, first_half, out_ref, scale):
    for j in range(NA_WIDTH // LANES):
        zj = z[:, j * LANES:(j + 1) * LANES]
        partner = jnp.where(first_half, pltpu.roll(zj, LANES - 16, 1), pltpu.roll(zj, 16, 1))
        r = zj * cos + partner * sin
        if scale != 1.0:
            r = r * scale
        out_ref[:, j * LANES:(j + 1) * LANES] = r.astype(out_ref.dtype)


def _inproj_kernel(*refs, mode):
    if mode == "latent":
        (x_ref, mod_ref, w_ref, lng_ref, lnb_ref, cos_ref, sin_ref,
         qr_ref, qp_ref, kr_ref, v_ref, gu_ref, svn_ref, sga_ref, sgb_ref) = refs
    elif mode == "ctx":
        (x_ref, mod_ref, w_ref, lng_ref, lnb_ref,
         qp_ref, kr_ref, v_ref, gu_ref, svn_ref, sga_ref, sgb_ref) = refs
    else:
        x_ref, mod_ref, w_ref, kr_ref, v_ref = refs

    scale = HEAD_DIM ** -0.5
    h = (x_ref[...] * (1.0 + mod_ref[1:2, :]) + mod_ref[0:1, :]).astype(BF16)

    def proj(lo, width):
        return _dot(h, w_ref[:, lo:lo + width])

    if mode == "ctx_kv":
        kr_ref[...] = proj(0, NA_WIDTH).astype(BF16)
        v_ref[...] = proj(NA_WIDTH, NA_WIDTH).astype(BF16)
        return

    zq = proj(0, NA_WIDTH)
    zk = proj(NA_WIDTH, NA_WIDTH)
    qp_ref[...] = (zq * scale).astype(BF16)
    if mode == "latent":
        cos = cos_ref[...]
        sin = sin_ref[...]
        lane = lax.broadcasted_iota(jnp.int32, cos.shape, 1)
        first_half = (lane % 32) < 16
        _rope_store(zq, cos, sin, first_half, qr_ref, scale)
        _rope_store(zk, cos, sin, first_half, kr_ref, 1.0)
    else:
        kr_ref[...] = zk.astype(BF16)
    v_ref[...] = proj(2 * NA_WIDTH, NA_WIDTH).astype(BF16)
    gu_ref[...] = jax.nn.gelu(proj(3 * NA_WIDTH, SG_WIDTH)).astype(BF16)
    sv = jax.nn.gelu(proj(3 * NA_WIDTH + SG_WIDTH, SG_WIDTH))
    svn_ref[...] = _layer_norm(sv, lng_ref[...], lnb_ref[...]).astype(BF16)
    lo = 3 * NA_WIDTH + 2 * SG_WIDTH
    sga_ref[...] = jax.nn.sigmoid(proj(lo, D_MODEL)).astype(BF16)
    sgb_ref[...] = jax.nn.sigmoid(proj(lo + D_MODEL, D_MODEL)).astype(BF16)


def _in_projection(x2d, mod, w, lng, lnb, rope, *, mode, tm, seq):
    t = x2d.shape[0]
    tiles_per_seq = seq // tm
    if mode == "latent":
        mod_map = lambda i: (i // tiles_per_seq, 0, 0)
    else:
        mod_map = lambda i: (CTX_MOD_ROW, 0, 0)
    tok = lambda width: pl.BlockSpec((tm, width), lambda i: (i, 0))
    in_specs = [tok(D_MODEL), pl.BlockSpec((None, 6, D_MODEL), mod_map), _resident(w.shape)]
    args = [x2d, mod, w]
    if mode != "ctx_kv":
        in_specs += [_resident((1, SG_WIDTH)), _resident((1, SG_WIDTH))]
        args += [lng, lnb]
    if mode == "latent":
        pos = pl.BlockSpec((tm, LANES), lambda i: (i % tiles_per_seq, 0))
        in_specs += [pos, pos]
        args += list(rope)
    widths = {"latent": [NA_WIDTH] * 6 + [D_MODEL] * 2,
              "ctx": [NA_WIDTH] * 5 + [D_MODEL] * 2,
              "ctx_kv": [NA_WIDTH] * 2}[mode]
    return pl.pallas_call(
        functools.partial(_inproj_kernel, mode=mode),
        grid=(t // tm,),
        in_specs=in_specs,
        out_specs=[tok(wd) for wd in widths],
        out_shape=[jax.ShapeDtypeStruct((t, wd), BF16) for wd in widths],
        compiler_params=_params(1),
        name="in_projection_" + mode,
    )(*args)


def _mix_kernel(*refs, latent):
    if latent:
        (x_ref, mod_ref, qr_ref, qp_ref, k0_ref, k1_ref, k2_ref, v0_ref, v1_ref, v2_ref,
         kc_ref, vc_ref, tab_ref, gu_ref, svn_ref, sga_ref, sgb_ref, ws_ref, bs_ref,
         wpa_ref, wpb_ref, wo_ref, lng_ref, lnb_ref, o_ref, oa_ref, ob_ref) = refs
        k_refs = (k0_ref, k1_ref, k2_ref)
        v_refs = (v0_ref, v1_ref, v2_ref)
    else:
        (x_ref, mod_ref, qp_ref, kc_ref, vc_ref, gu_ref, svn_ref, sga_ref, sgb_ref,
         ws_ref, bs_ref, wpa_ref, wpb_ref, wo_ref, lng_ref, lnb_ref,
         o_ref, oa_ref, ob_ref) = refs
        k_refs = v_refs = ()

    lane = lax.broadcasted_iota(jnp.int32, (Q_BLOCK, LANES), 1)
    low_half = lane < HEAD_DIM

    for p in range(HEAD_PAIRS):
        cols = slice(p * LANES, (p + 1) * LANES)
        q_ctx = qp_ref[:, cols]
        kc = kc_ref[:, cols]
        vc = vc_ref[:, cols]
        outs = []
        for e in range(2):
            mine = low_half if e == 0 else jnp.logical_not(low_half)
            s_parts = [_dot_nt(jnp.where(mine, q_ctx, 0), kc)]
            if latent:
                q_lat = jnp.where(mine, qr_ref[:, cols], 0)
                for m in range(3):
                    s = _dot_nt(q_lat, k_refs[m][:, cols])
                    s_parts.append(s + tab_ref[2 * p + e, :, m * Q_BLOCK:(m + 1) * Q_BLOCK])
            mx = s_parts[0].max(axis=-1, keepdims=True)
            for s in s_parts[1:]:
                mx = jnp.maximum(mx, s.max(axis=-1, keepdims=True))
            p_parts = [jnp.exp(s - mx) for s in s_parts]
            denom = p_parts[0].sum(axis=-1, keepdims=True)
            for pp in p_parts[1:]:
                denom = denom + pp.sum(axis=-1, keepdims=True)
            acc = _dot(p_parts[0].astype(BF16), vc)
            for m in range(len(p_parts) - 1):
                acc = acc + _dot(p_parts[m + 1].astype(BF16), v_refs[m][:, cols])
            outs.append(acc / denom)
        oa_ref[:, cols] = jnp.where(low_half, outs[0], outs[1]).astype(BF16)

    half = lax.broadcasted_iota(jnp.int32, (CHUNK, LANES), 1) < (SG_WIDTH // SG_GROUPS)
    for c in range(Q_BLOCK // CHUNK):
        rows = slice(c * CHUNK, (c + 1) * CHUNK)
        for p in range(SG_GROUPS // 2):
            cols = slice(p * LANES, (p + 1) * LANES)
            vl = svn_ref[rows, cols]
            mixed = jnp.where(half, _dot(ws_ref[2 * p], vl), _dot(ws_ref[2 * p + 1], vl))
            mixed = mixed + bs_ref[p]
            ob_ref[rows, cols] = (gu_ref[rows, cols].astype(F32) * mixed).astype(BF16)

    ya = _dot(oa_ref[...], wpa_ref[...])
    yb = _dot(ob_ref[...], wpb_ref[...])
    y = sga_ref[...].astype(F32) * ya + sgb_ref[...].astype(F32) * yb
    out = _dot(y.astype(BF16), wo_ref[...])
    r = ALPHA * x_ref[...] + mod_ref[2:3, :] * out
    o_ref[...] = _layer_norm(r, lng_ref[...], lnb_ref[...])


def _mix(x2d, mod, acts, ctx_kv, tab, wts, *, latent):
    t = x2d.shape[0]
    blocks_per_seq = (SEQ if latent else CTX_LEN) // Q_BLOCK
    tok = lambda width: pl.BlockSpec((Q_BLOCK, width), lambda b, j: (b * blocks_per_seq + j, 0))
    ctx_spec = pl.BlockSpec((CTX_LEN, NA_WIDTH), lambda b, j: (b, 0))
    ws, bs, wpa, wpb, wo, lng, lnb = wts
    w_specs = [_resident(a.shape) for a in wts]
    if latent:
        qr, qp, kr, v, gu, svn, sga, sgb = acts
        kc, vc = ctx_kv
        mod_map = lambda b, j: (b, 0, 0)

        def nbr(m):
            return pl.BlockSpec(
                (Q_BLOCK, NA_WIDTH),
                lambda b, j: (b * blocks_per_seq + jnp.clip(j - 1 + m, 0, blocks_per_seq - 1), 0))

        last = blocks_per_seq - 1
        tab_spec = pl.BlockSpec(
            (None, NA_HEADS, Q_BLOCK, K_ROWS * GRID_W),
            lambda b, j: (jnp.where(j == 0, 0, jnp.where(j == last, 2, 1)), 0, 0, 0))
        in_specs = ([tok(D_MODEL), pl.BlockSpec((None, 6, D_MODEL), mod_map),
                     tok(NA_WIDTH), tok(NA_WIDTH)]
                    + [nbr(m) for m in range(3)] + [nbr(m) for m in range(3)]
                    + [ctx_spec, ctx_spec, tab_spec]
                    + [tok(NA_WIDTH), tok(NA_WIDTH), tok(D_MODEL), tok(D_MODEL)] + w_specs)
        args = [x2d, mod, qr, qp, kr, kr, kr, v, v, v, kc, vc, tab, gu, svn, sga, sgb] + list(wts)
    else:
        qp, kc, vc, gu, svn, sga, sgb = acts
        mod_map = lambda b, j: (CTX_MOD_ROW, 0, 0)
        in_specs = ([tok(D_MODEL), pl.BlockSpec((None, 6, D_MODEL), mod_map),
                     tok(NA_WIDTH), ctx_spec, ctx_spec]
                    + [tok(NA_WIDTH), tok(NA_WIDTH), tok(D_MODEL), tok(D_MODEL)] + w_specs)
        args = [x2d, mod, qp, kc, vc, gu, svn, sga, sgb] + list(wts)
    return pl.pallas_call(
        functools.partial(_mix_kernel, latent=latent),
        grid=(BATCH, blocks_per_seq),
        in_specs=in_specs,
        out_specs=tok(D_MODEL),
        out_shape=jax.ShapeDtypeStruct((t, D_MODEL), F32),
        scratch_shapes=[pltpu.VMEM((Q_BLOCK, NA_WIDTH), BF16),
                        pltpu.VMEM((Q_BLOCK, SG_WIDTH), BF16)],
        compiler_params=_params(2),
        name="mix_latent" if latent else "mix_ctx",
    )(*args)


def _ffn_kernel(x_ref, prev_ref, next_ref, mod_ref, wa_ref, wg_ref, cp_ref, wd_ref,
                lng_ref, lnb_ref, o_ref, za_ref, zg_ref, acc_ref, *, tm, tiles_per_seq):
    i = pl.program_id(0)
    has_prev = (i % tiles_per_seq != 0).astype(F32)
    has_next = (i % tiles_per_seq != tiles_per_seq - 1).astype(F32)
    shift = mod_ref[3:4, :]
    scale1 = 1.0 + mod_ref[4:5, :]
    x = x_ref[...]
    h = (x * scale1 + shift).astype(BF16)
    edge = jnp.concatenate([(prev_ref[...] * scale1 + shift) * has_prev,
                            (next_ref[...] * scale1 + shift) * has_next], axis=0).astype(BF16)
    acc_ref[...] = jnp.zeros_like(acc_ref)

    def conv(z_ref, w_ref, cp, row):
        z_ref[HALO:HALO + tm, :] = _dot(h, w_ref)
        ze = _dot(edge, w_ref)
        z_ref[0:HALO, :] = ze[0:HALO]
        z_ref[HALO + tm:2 * HALO + tm, :] = ze[HALO:2 * HALO]
        return (cp[row:row + 1] * z_ref[HALO - 1:HALO - 1 + tm, :]
                + cp[row + 1:row + 2] * z_ref[HALO:HALO + tm, :]
                + cp[row + 2:row + 3] * z_ref[HALO + 1:HALO + 1 + tm, :]
                + cp[row + 3:row + 4])

    def body(c, carry):
        cp = cp_ref[c]
        a = conv(za_ref, wa_ref[c], cp, 0)
        g = conv(zg_ref, wg_ref[c], cp, 4)
        act = (jax.nn.silu(g) * a).astype(BF16)
        acc_ref[...] += _dot(act, wd_ref[c])
        return carry

    lax.fori_loop(0, N_FF_CHUNKS, body, 0)
    r = ALPHA * x + mod_ref[5:6, :] * acc_ref[...]
    o_ref[...] = _layer_norm(r, lng_ref[...], lnb_ref[...])


def _conv_ffn(x2d, mod, wts, *, latent, tm, seq):
    t = x2d.shape[0]
    tiles_per_seq = seq // tm
    n_halo_blocks = t // HALO
    per_tile = tm // HALO
    if latent:
        mod_map = lambda i: (i // tiles_per_seq, 0, 0)
    else:
        mod_map = lambda i: (CTX_MOD_ROW, 0, 0)
    tok = pl.BlockSpec((tm, D_MODEL), lambda i: (i, 0))
    prev = pl.BlockSpec((HALO, D_MODEL), lambda i: (jnp.maximum(i * per_tile - 1, 0), 0))
    nxt = pl.BlockSpec((HALO, D_MODEL),
                       lambda i: (jnp.minimum((i + 1) * per_tile, n_halo_blocks - 1), 0))
    return pl.pallas_call(
        functools.partial(_ffn_kernel, tm=tm, tiles_per_seq=tiles_per_seq),
        grid=(t // tm,),
        in_specs=[tok, prev, nxt, pl.BlockSpec((None, 6, D_MODEL), mod_map)]
                 + [_resident(a.shape) for a in wts],
        out_specs=tok,
        out_shape=jax.ShapeDtypeStruct((t, D_MODEL), F32),
        scratch_shapes=[pltpu.VMEM((tm + 2 * HALO, FF_CHUNK), F32),
                        pltpu.VMEM((tm + 2 * HALO, FF_CHUNK), F32),
                        pltpu.VMEM((tm, D_MODEL), F32)],
        compiler_params=_params(1),
        name="conv_ffn_latent" if latent else "conv_ffn_ctx",
    )(x2d, x2d, x2d, mod, *wts)


def _rope_tables():
    half = HEAD_DIM // 2
    nf = half // 2
    inv = ROPE_BASE ** (-jnp.arange(nf, dtype=F32) / nf)
    t = jnp.arange(SEQ)

    def part(pos):
        ang = pos.astype(F32)[:, None] * inv[None, :]
        cos = jnp.cos(ang)
        sin = jnp.sin(ang)
        return jnp.concatenate([cos, cos], axis=-1), jnp.concatenate([-sin, sin], axis=-1)

    cr, sr = part(t // GRID_W)
    cc, sc = part(t % GRID_W)
    cos = jnp.concatenate([cr, cc], axis=-1)
    sin = jnp.concatenate([sr, sc], axis=-1)
    reps = LANES // HEAD_DIM
    return jnp.tile(cos, (1, reps)), jnp.tile(sin, (1, reps))


def _row_ok(t, qi, ki):
    if t == 0:
        return ki >= Q_ROWS
    if t == 2:
        return ki < WIN_H
    return qi <= ki < qi + WIN_H


def _bias_kernel(rpb_ref, o_ref, c_ref):
    n_dy, n_dx = 2 * WIN_H - 1, 2 * WIN_W - 1
    head = pl.program_id(0) * NA_HEADS + pl.program_id(1)
    qc = lax.broadcasted_iota(jnp.int32, (GRID_W, GRID_W), 0)
    kc = lax.broadcasted_iota(jnp.int32, (GRID_W, GRID_W), 1)
    win_start = jnp.clip(qc - WIN_W // 2, 0, GRID_W - WIN_W)
    col_ok = (kc >= win_start) & (kc < win_start + WIN_W)
    dx = kc - qc + WIN_W - 1
    masked = jnp.full((GRID_W, GRID_W), NEG_INF, F32)

    def toeplitz(dy, carry):
        acc = masked
        for b in range(n_dx):
            acc = jnp.where(col_ok & (dx == b), rpb_ref[(head * n_dy + dy) * n_dx + b], acc)
        c_ref[dy] = acc
        return carry

    lax.fori_loop(0, n_dy, toeplitz, 0)
    for t in range(3):
        for qi in range(Q_ROWS):
            for kp in range(K_ROWS // 2):
                pair = [c_ref[ki - qi + WIN_H // 2 - 1] if _row_ok(t, qi, ki) else masked
                        for ki in (2 * kp, 2 * kp + 1)]
                o_ref[t, qi * GRID_W:(qi + 1) * GRID_W, kp * LANES:(kp + 1) * LANES] = (
                    jnp.concatenate(pair, axis=1))


def _bias_tables(rpb):
    n_keys = K_ROWS * GRID_W
    return pl.pallas_call(
        _bias_kernel,
        grid=(DEPTH, NA_HEADS),
        in_specs=[pl.BlockSpec(memory_space=pltpu.SMEM)],
        out_specs=pl.BlockSpec((None, 3, None, Q_BLOCK, n_keys), lambda l, h: (l, 0, h, 0, 0)),
        out_shape=jax.ShapeDtypeStruct((DEPTH, 3, NA_HEADS, Q_BLOCK, n_keys), F32),
        scratch_shapes=[pltpu.VMEM((2 * WIN_H - 1, GRID_W, GRID_W), F32)],
        compiler_params=_params(2),
        name="bias_tables",
    )(rpb.reshape(-1))


def _chunk_cols(w):
    return w.reshape(w.shape[0], N_FF_CHUNKS, FF_CHUNK).transpose(1, 0, 2)


def _layer_weights(i, w_in, sg_ln_g, sg_ln_b, w_s, b_s, w_pa, w_pb, w_o, ln1_g, ln1_b,
                   w_up, conv_w, conv_b, w_down, ln2_g, ln2_b):
    row = lambda a: a.reshape(1, -1)
    bs = jnp.repeat(b_s[i].reshape(SG_GROUPS // 2, 2, CHUNK).transpose(0, 2, 1),
                    SG_WIDTH // SG_GROUPS, axis=-1)
    mix = (w_s[i].astype(BF16), bs, w_pa[i].astype(BF16), w_pb[i].astype(BF16),
           w_o[i].astype(BF16), row(ln1_g[i]), row(ln1_b[i]))
    cw, cb = conv_w[i], conv_b[i]
    cp = jnp.concatenate([cw[:, :D_FF], cb[None, :D_FF], cw[:, D_FF:], cb[None, D_FF:]], axis=0)
    cp = cp.reshape(8, N_FF_CHUNKS, FF_CHUNK).transpose(1, 0, 2)
    ffn = (_chunk_cols(w_up[i][:, :D_FF]).astype(BF16), _chunk_cols(w_up[i][:, D_FF:]).astype(BF16),
           cp, w_down[i].astype(BF16).reshape(N_FF_CHUNKS, FF_CHUNK, D_MODEL),
           row(ln2_g[i]), row(ln2_b[i]))
    return w_in[i].astype(BF16), row(sg_ln_g[i]), row(sg_ln_b[i]), mix, ffn


def kernel(x, c, ctx, c_ctx, w_ada, b_ada, w_in, rpb, sg_ln_g, sg_ln_b, w_s, b_s, w_pa, w_pb, w_o,
           ln1_g, ln1_b, w_up, conv_w, conv_b, w_down, ln2_g, ln2_b):
    assert x.shape == (BATCH, SEQ, D_MODEL) and ctx.shape == (BATCH, CTX_LEN, D_MODEL)
    cc = jnp.concatenate([c, c_ctx[None, :], jnp.zeros((MOD_ROWS - BATCH - 1, D_MODEL), F32)], axis=0)
    mods = _modulation(cc, w_ada, b_ada)
    rope = _rope_tables()
    tabs = _bias_tables(rpb)
    xl = x.reshape(BATCH * SEQ, D_MODEL)
    xc = ctx.reshape(BATCH * CTX_LEN, D_MODEL)
    for i in range(DEPTH):
        w_in_i, lng, lnb, mix_w, ffn_w = _layer_weights(
            i, w_in, sg_ln_g, sg_ln_b, w_s, b_s, w_pa, w_pb, w_o, ln1_g, ln1_b,
            w_up, conv_w, conv_b, w_down, ln2_g, ln2_b)
        mod = mods[i]
        tab = tabs[i]
        if i < DEPTH - 1:
            c_acts = _in_projection(xc, mod, w_in_i, lng, lnb, None, mode="ctx", tm=CTX_LEN, seq=CTX_LEN)
            ctx_kv = (c_acts[1], c_acts[2])
        else:
            ctx_kv = _in_projection(xc, mod, w_in_i[:, NA_WIDTH:3 * NA_WIDTH], None, None, None,
                                    mode="ctx_kv", tm=CTX_LEN, seq=CTX_LEN)
        acts = _in_projection(xl, mod, w_in_i, lng, lnb, rope, mode="latent", tm=512, seq=SEQ)
        xl = _mix(xl, mod, acts, ctx_kv, tab, mix_w, latent=True)
        xl = _conv_ffn(xl, mod, ffn_w, latent=True, tm=512, seq=SEQ)
        if i < DEPTH - 1:
            xc = _mix(xc, mod, c_acts, None, None, mix_w, latent=False)
            xc = _conv_ffn(xc, mod, ffn_w, latent=False, tm=CTX_LEN, seq=CTX_LEN)
    return xl.reshape(BATCH, SEQ, D_MODEL)
```

```python
import functools

import jax
import jax.numpy as jnp
from jax import lax
from jax.experimental import pallas as pl
from jax.experimental.pallas import tpu as pltpu

D_MODEL = 1024
BATCH = 4
SEQ = 4096
DEPTH = 2
GRID_W = 64
CTX_LEN = 256
NA_HEADS = 8
HEAD_DIM = 64
NA_WIDTH = NA_HEADS * HEAD_DIM
WIN_H = 8
WIN_W = 16
ROPE_BASE = 10000.0
SG_GROUPS = 8
SG_WIDTH = 512
CHUNK = 128
D_FF = 2816
N_IN = 3 * NA_WIDTH + 2 * SG_WIDTH + 2 * D_MODEL
ALPHA = (2 * DEPTH) ** 0.25
LN_EPS = 1e-5
NEG_INF = -1e30

F32 = jnp.float32
BF16 = jnp.bfloat16

LANES = 128
HEAD_PAIRS = NA_HEADS // 2
MOD_ROWS = 8
CTX_MOD_ROW = BATCH
Q_ROWS = 4
Q_BLOCK = Q_ROWS * GRID_W
K_ROWS = 3 * Q_ROWS
FF_CHUNK = 256
N_FF_CHUNKS = D_FF // FF_CHUNK
HALO = 8
VMEM_LIMIT = 56 * 1024 * 1024


def _dot(a, b):
    return jnp.dot(a, b, preferred_element_type=F32)


def _dot_nt(a, b):
    return lax.dot_general(a, b, (((1,), (1,)), ((), ())), preferred_element_type=F32)


def _layer_norm(v, g, b):
    mu = jnp.mean(v, axis=-1, keepdims=True)
    d = v - mu
    var = jnp.mean(d * d, axis=-1, keepdims=True)
    return d * lax.rsqrt(var + LN_EPS) * g + b


def _params(n_axes, flags=None):
    return pltpu.CompilerParams(dimension_semantics=("arbitrary",) * n_axes,
                                vmem_limit_bytes=VMEM_LIMIT, flags=flags)


def _resident(shape):
    zeros = (0,) * len(shape)
    return pl.BlockSpec(shape, lambda *_: zeros, pipeline_mode=pl.Buffered(1))


def _mod_kernel(cc_ref, w_ref, b_ref, o_ref):
    s = jax.nn.silu(cc_ref[...]).astype(BF16)
    o_ref[...] = _dot(s, w_ref[...].astype(BF16)) + b_ref[...]


def _modulation(cc, w_ada, b_ada):
    tn = 1536
    n6 = 6 * D_MODEL
    out = pl.pallas_call(
        _mod_kernel,
        grid=(DEPTH, n6 // tn),
        in_specs=[pl.BlockSpec((MOD_ROWS, D_MODEL), lambda l, n: (0, 0)),
                  pl.BlockSpec((None, D_MODEL, tn), lambda l, n: (l, 0, n)),
                  pl.BlockSpec((None, 1, tn), lambda l, n: (l, 0, n))],
        out_specs=pl.BlockSpec((None, MOD_ROWS, tn), lambda l, n: (l, 0, n)),
        out_shape=jax.ShapeDtypeStruct((DEPTH, MOD_ROWS, n6), F32),
        compiler_params=_params(2),
        name="adaln_modulation",
    )(cc, w_ada, b_ada.reshape(DEPTH, 1, n6))
    return out.reshape(DEPTH, MOD_ROWS, 6, D_MODEL)


def _rope_store(z, cos, sin, first_half, out_ref, scale):
    for j in range(NA_WIDTH // LANES):
        zj = z[:, j * LANES:(j + 1) * LANES]
        partner = jnp.where(first_half, pltpu.roll(zj, LANES - 16, 1), pltpu.roll(zj, 16, 1))
        r = zj * cos + partner * sin
        if scale != 1.0:
            r = r * scale
        out_ref[:, j * LANES:(j + 1) * LANES] = r.astype(out_ref.dtype)


def _inproj_kernel(*refs, mode):
    if mode == "latent":
        (x_ref, mod_ref, w_ref, lng_ref, lnb_ref, cos_ref, sin_ref,
         qr_ref, qp_ref, kr_ref, v_ref, gu_ref, svn_ref, sga_ref, sgb_ref) = refs
    elif mode == "ctx":
        (x_ref, mod_ref, w_ref, lng_ref, lnb_ref,
         qp_ref, kr_ref, v_ref, gu_ref, svn_ref, sga_ref, sgb_ref) = refs
    else:
        x_ref, mod_ref, w_ref, kr_ref, v_ref = refs

    scale = HEAD_DIM ** -0.5
    h = (x_ref[...] * (1.0 + mod_ref[1:2, :]) + mod_ref[0:1, :]).astype(BF16)

    def proj(lo, width):
        return _dot(h, w_ref[:, lo:lo + width])

    if mode == "ctx_kv":
        kr_ref[...] = proj(0, NA_WIDTH).astype(BF16)
        v_ref[...] = proj(NA_WIDTH, NA_WIDTH).astype(BF16)
        return

    zq = proj(0, NA_WIDTH)
    zk = proj(NA_WIDTH, NA_WIDTH)
    qp_ref[...] = (zq * scale).astype(BF16)
    if mode == "latent":
        cos = cos_ref[...]
        sin = sin_ref[...]
        lane = lax.broadcasted_iota(jnp.int32, cos.shape, 1)
        first_half = (lane % 32) < 16
        _rope_store(zq, cos, sin, first_half, qr_ref, scale)
        _rope_store(zk, cos, sin, first_half, kr_ref, 1.0)
    else:
        kr_ref[...] = zk.astype(BF16)
    v_ref[...] = proj(2 * NA_WIDTH, NA_WIDTH).astype(BF16)
    gu_ref[...] = jax.nn.gelu(proj(3 * NA_WIDTH, SG_WIDTH)).astype(BF16)
    sv = jax.nn.gelu(proj(3 * NA_WIDTH + SG_WIDTH, SG_WIDTH))
    svn_ref[...] = _layer_norm(sv, lng_ref[...], lnb_ref[...]).astype(BF16)
    lo = 3 * NA_WIDTH + 2 * SG_WIDTH
    sga_ref[...] = jax.nn.sigmoid(proj(lo, D_MODEL)).astype(BF16)
    sgb_ref[...] = jax.nn.sigmoid(proj(lo + D_MODEL, D_MODEL)).astype(BF16)


def _in_projection(x2d, mod, w, lng, lnb, rope, *, mode, tm, seq):
    t = x2d.shape[0]
    tiles_per_seq = seq // tm
    if mode == "latent":
        mod_map = lambda i: (i // tiles_per_seq, 0, 0)
    else:
        mod_map = lambda i: (CTX_MOD_ROW, 0, 0)
    tok = lambda width: pl.BlockSpec((tm, width), lambda i: (i, 0))
    in_specs = [tok(D_MODEL), pl.BlockSpec((None, 6, D_MODEL), mod_map), _resident(w.shape)]
    args = [x2d, mod, w]
    if mode != "ctx_kv":
        in_specs += [_resident((1, SG_WIDTH)), _resident((1, SG_WIDTH))]
        args += [lng, lnb]
    if mode == "latent":
        pos = pl.BlockSpec((tm, LANES), lambda i: (i % tiles_per_seq, 0))
        in_specs += [pos, pos]
        args += list(rope)
    widths = {"latent": [NA_WIDTH] * 6 + [D_MODEL] * 2,
              "ctx": [NA_WIDTH] * 5 + [D_MODEL] * 2,
              "ctx_kv": [NA_WIDTH] * 2}[mode]
    return pl.pallas_call(
        functools.partial(_inproj_kernel, mode=mode),
        grid=(t // tm,),
        in_specs=in_specs,
        out_specs=[tok(wd) for wd in widths],
        out_shape=[jax.ShapeDtypeStruct((t, wd), BF16) for wd in widths],
        compiler_params=_params(1),
        name="in_projection_" + mode,
    )(*args)


def _mix_kernel(*refs, latent):
    if latent:
        (x_ref, mod_ref, qr_ref, qp_ref, k0_ref, k1_ref, k2_ref, v0_ref, v1_ref, v2_ref,
         kc_ref, vc_ref, tab_ref, gu_ref, svn_ref, sga_ref, sgb_ref, ws_ref, bs_ref,
         wpa_ref, wpb_ref, wo_ref, lng_ref, lnb_ref, o_ref, oa_ref, ob_ref) = refs
        k_refs = (k0_ref, k1_ref, k2_ref)
        v_refs = (v0_ref, v1_ref, v2_ref)
    else:
        (x_ref, mod_ref, qp_ref, kc_ref, vc_ref, gu_ref, svn_ref, sga_ref, sgb_ref,
         ws_ref, bs_ref, wpa_ref, wpb_ref, wo_ref, lng_ref, lnb_ref,
         o_ref, oa_ref, ob_ref) = refs
        k_refs = v_refs = ()

    lane = lax.broadcasted_iota(jnp.int32, (Q_BLOCK, LANES), 1)
    low_half = lane < HEAD_DIM

    for p in range(HEAD_PAIRS):
        cols = slice(p * LANES, (p + 1) * LANES)
        q_ctx = qp_ref[:, cols]
        kc = kc_ref[:, cols]
        vc = vc_ref[:, cols]
        outs = []
        for e in range(2):
            mine = low_half if e == 0 else jnp.logical_not(low_half)
            s_parts = [_dot_nt(jnp.where(mine, q_ctx, 0), kc)]
            if latent:
                q_lat = jnp.where(mine, qr_ref[:, cols], 0)
                for m in range(3):
                    s = _dot_nt(q_lat, k_refs[m][:, cols])
                    s_parts.append(s + tab_ref[2 * p + e, :, m * Q_BLOCK:(m + 1) * Q_BLOCK])
            mx = functools.reduce(jnp.maximum, s_parts).max(axis=-1, keepdims=True)
            p_parts = [jnp.exp(s - mx) for s in s_parts]
            denom = functools.reduce(jnp.add, p_parts).sum(axis=-1, keepdims=True)
            acc = _dot(p_parts[0].astype(BF16), vc)
            for m in range(len(p_parts) - 1):
                acc = acc + _dot(p_parts[m + 1].astype(BF16), v_refs[m][:, cols])
            outs.append(acc / denom)
        oa_ref[:, cols] = jnp.where(low_half, outs[0], outs[1]).astype(BF16)

    half = lax.broadcasted_iota(jnp.int32, (CHUNK, LANES), 1) < (SG_WIDTH // SG_GROUPS)
    for c in range(Q_BLOCK // CHUNK):
        rows = slice(c * CHUNK, (c + 1) * CHUNK)
        for p in range(SG_GROUPS // 2):
            cols = slice(p * LANES, (p + 1) * LANES)
            vl = svn_ref[rows, cols]
            mixed = jnp.where(half, _dot(ws_ref[2 * p], vl), _dot(ws_ref[2 * p + 1], vl))
            mixed = mixed + bs_ref[p]
            ob_ref[rows, cols] = (gu_ref[rows, cols].astype(F32) * mixed).astype(BF16)

    ya = _dot(oa_ref[...], wpa_ref[...])
    yb = _dot(ob_ref[...], wpb_ref[...])
    y = sga_ref[...].astype(F32) * ya + sgb_ref[...].astype(F32) * yb
    out = _dot(y.astype(BF16), wo_ref[...])
    r = ALPHA * x_ref[...] + mod_ref[2:3, :] * out
    o_ref[...] = _layer_norm(r, lng_ref[...], lnb_ref[...])


def _mix(x2d, mod, acts, ctx_kv, tab, wts, *, latent):
    t = x2d.shape[0]
    blocks_per_seq = (SEQ if latent else CTX_LEN) // Q_BLOCK
    tok = lambda width: pl.BlockSpec((Q_BLOCK, width), lambda b, j: (b * blocks_per_seq + j, 0))
    ctx_spec = pl.BlockSpec((CTX_LEN, NA_WIDTH), lambda b, j: (b, 0))
    ws, bs, wpa, wpb, wo, lng, lnb = wts
    w_specs = [_resident(a.shape) for a in wts]
    if latent:
        qr, qp, kr, v, gu, svn, sga, sgb = acts
        kc, vc = ctx_kv
        mod_map = lambda b, j: (b, 0, 0)

        def nbr(m):
            return pl.BlockSpec(
                (Q_BLOCK, NA_WIDTH),
                lambda b, j: (b * blocks_per_seq + jnp.clip(j - 1 + m, 0, blocks_per_seq - 1), 0))

        last = blocks_per_seq - 1
        tab_spec = pl.BlockSpec(
            (None, NA_HEADS, Q_BLOCK, K_ROWS * GRID_W),
            lambda b, j: (jnp.where(j == 0, 0, jnp.where(j == last, 2, 1)), 0, 0, 0))
        in_specs = ([tok(D_MODEL), pl.BlockSpec((None, 6, D_MODEL), mod_map),
                     tok(NA_WIDTH), tok(NA_WIDTH)]
                    + [nbr(m) for m in range(3)] + [nbr(m) for m in range(3)]
                    + [ctx_spec, ctx_spec, tab_spec]
                    + [tok(NA_WIDTH), tok(NA_WIDTH), tok(D_MODEL), tok(D_MODEL)] + w_specs)
        args = [x2d, mod, qr, qp, kr, kr, kr, v, v, v, kc, vc, tab, gu, svn, sga, sgb] + list(wts)
    else:
        qp, kc, vc, gu, svn, sga, sgb = acts
        mod_map = lambda b, j: (CTX_MOD_ROW, 0, 0)
        in_specs = ([tok(D_MODEL), pl.BlockSpec((None, 6, D_MODEL), mod_map),
                     tok(NA_WIDTH), ctx_spec, ctx_spec]
                    + [tok(NA_WIDTH), tok(NA_WIDTH), tok(D_MODEL), tok(D_MODEL)] + w_specs)
        args = [x2d, mod, qp, kc, vc, gu, svn, sga, sgb] + list(wts)
    return pl.pallas_call(
        functools.partial(_mix_kernel, latent=latent),
        grid=(BATCH, blocks_per_seq),
        in_specs=in_specs,
        out_specs=tok(D_MODEL),
        out_shape=jax.ShapeDtypeStruct((t, D_MODEL), F32),
        scratch_shapes=[pltpu.VMEM((Q_BLOCK, NA_WIDTH), BF16),
                        pltpu.VMEM((Q_BLOCK, SG_WIDTH), BF16)],
        compiler_params=_params(2),
        name="mix_latent" if latent else "mix_ctx",
    )(*args)


def _ffn_kernel(x_ref, prev_ref, next_ref, mod_ref, wa_ref, wg_ref, cp_ref, wd_ref,
                lng_ref, lnb_ref, o_ref, h_ref, act_ref, f_ref, *, tm, tiles_per_seq):
    i = pl.program_id(0)
    has_prev = (i % tiles_per_seq != 0).astype(F32)
    has_next = (i % tiles_per_seq != tiles_per_seq - 1).astype(F32)
    shift = mod_ref[3:4, :]
    scale1 = 1.0 + mod_ref[4:5, :]
    rows = tm + 2 * HALO
    h_ref[0:tm, :] = (x_ref[...] * scale1 + shift).astype(BF16)
    h_ref[tm:rows, :] = jnp.concatenate(
        [(next_ref[...] * scale1 + shift) * has_next,
         (prev_ref[...] * scale1 + shift) * has_prev], axis=0).astype(BF16)

    def conv(w_ref, cp, row):
        z = _dot(h_ref[...], w_ref)
        before = pltpu.roll(z, 1, 0)[0:tm]
        after = pltpu.roll(z, rows - 1, 0)[0:tm]
        return (cp[row:row + 1] * before + cp[row + 1:row + 2] * z[0:tm]
                + cp[row + 2:row + 3] * after + cp[row + 3:row + 4])

    def down(c):
        return _dot(act_ref[c], wd_ref[c])

    assert N_FF_CHUNKS % 2 == 1
    for c in range(N_FF_CHUNKS):
        cp = cp_ref[c]
        act_ref[c] = (jax.nn.silu(conv(wg_ref[c], cp, 4)) * conv(wa_ref[c], cp, 0)).astype(BF16)
        if c == 1:
            f_ref[...] = down(0) + down(1)
        elif c % 2 == 1:
            f_ref[...] += down(c - 1) + down(c)
    f = f_ref[...] + down(N_FF_CHUNKS - 1)
    r = ALPHA * x_ref[...] + mod_ref[5:6, :] * f
    o_ref[...] = _layer_norm(r, lng_ref[...], lnb_ref[...])


def _conv_ffn(x2d, mod, wts, *, latent, tm, seq):
    t = x2d.shape[0]
    tiles_per_seq = seq // tm
    n_halo_blocks = t // HALO
    per_tile = tm // HALO
    if latent:
        mod_map = lambda i: (i // tiles_per_seq, 0, 0)
    else:
        mod_map = lambda i: (CTX_MOD_ROW, 0, 0)
    tok = pl.BlockSpec((tm, D_MODEL), lambda i: (i, 0))
    prev = pl.BlockSpec((HALO, D_MODEL), lambda i: (jnp.maximum(i * per_tile - 1, 0), 0))
    nxt = pl.BlockSpec((HALO, D_MODEL),
                       lambda i: (jnp.minimum((i + 1) * per_tile, n_halo_blocks - 1), 0))
    return pl.pallas_call(
        functools.partial(_ffn_kernel, tm=tm, tiles_per_seq=tiles_per_seq),
        grid=(t // tm,),
        in_specs=[tok, prev, nxt, pl.BlockSpec((None, 6, D_MODEL), mod_map)]
                 + [_resident(a.shape) for a in wts],
        out_specs=tok,
        out_shape=jax.ShapeDtypeStruct((t, D_MODEL), F32),
        scratch_shapes=[pltpu.VMEM((tm + 2 * HALO, D_MODEL), BF16),
                        pltpu.VMEM((N_FF_CHUNKS, tm, FF_CHUNK), BF16),
                        pltpu.VMEM((tm, D_MODEL), F32)],
        compiler_params=_params(1),
        name="conv_ffn_latent" if latent else "conv_ffn_ctx",
    )(x2d, x2d, x2d, mod, *wts)


def _rope_tables():
    half = HEAD_DIM // 2
    nf = half // 2
    inv = ROPE_BASE ** (-jnp.arange(nf, dtype=F32) / nf)
    t = jnp.arange(SEQ)

    def part(pos):
        ang = pos.astype(F32)[:, None] * inv[None, :]
        cos = jnp.cos(ang)
        sin = jnp.sin(ang)
        return jnp.concatenate([cos, cos], axis=-1), jnp.concatenate([-sin, sin], axis=-1)

    cr, sr = part(t // GRID_W)
    cc, sc = part(t % GRID_W)
    cos = jnp.concatenate([cr, cc], axis=-1)
    sin = jnp.concatenate([sr, sc], axis=-1)
    reps = LANES // HEAD_DIM
    return jnp.tile(cos, (1, reps)), jnp.tile(sin, (1, reps))


def _row_ok(t, qi, ki):
    if t == 0:
        return ki >= Q_ROWS
    if t == 2:
        return ki < WIN_H
    return qi <= ki < qi + WIN_H


def _bias_kernel(rpb_ref, o_ref, c_ref):
    n_dy, n_dx = 2 * WIN_H - 1, 2 * WIN_W - 1
    head = pl.program_id(0) * NA_HEADS + pl.program_id(1)
    qc = lax.broadcasted_iota(jnp.int32, (GRID_W, GRID_W), 0)
    kc = lax.broadcasted_iota(jnp.int32, (GRID_W, GRID_W), 1)
    win_start = jnp.clip(qc - WIN_W // 2, 0, GRID_W - WIN_W)
    col_ok = (kc >= win_start) & (kc < win_start + WIN_W)
    dx = kc - qc + WIN_W - 1
    masked = jnp.full((GRID_W, GRID_W), NEG_INF, F32)

    def toeplitz(dy, carry):
        acc = masked
        for b in range(n_dx):
            acc = jnp.where(col_ok & (dx == b), rpb_ref[(head * n_dy + dy) * n_dx + b], acc)
        c_ref[dy] = acc
        return carry

    lax.fori_loop(0, n_dy, toeplitz, 0)
    for t in range(3):
        for qi in range(Q_ROWS):
            for kp in range(K_ROWS // 2):
                pair = [c_ref[ki - qi + WIN_H // 2 - 1] if _row_ok(t, qi, ki) else masked
                        for ki in (2 * kp, 2 * kp + 1)]
                o_ref[t, qi * GRID_W:(qi + 1) * GRID_W, kp * LANES:(kp + 1) * LANES] = (
                    jnp.concatenate(pair, axis=1))


def _bias_tables(rpb):
    n_keys = K_ROWS * GRID_W
    return pl.pallas_call(
        _bias_kernel,
        grid=(DEPTH, NA_HEADS),
        in_specs=[pl.BlockSpec(memory_space=pltpu.SMEM)],
        out_specs=pl.BlockSpec((None, 3, None, Q_BLOCK, n_keys), lambda l, h: (l, 0, h, 0, 0)),
        out_shape=jax.ShapeDtypeStruct((DEPTH, 3, NA_HEADS, Q_BLOCK, n_keys), F32),
        scratch_shapes=[pltpu.VMEM((2 * WIN_H - 1, GRID_W, GRID_W), F32)],
        compiler_params=_params(2),
        name="bias_tables",
    )(rpb.reshape(-1))


def _chunk_cols(w):
    return w.reshape(w.shape[0], N_FF_CHUNKS, FF_CHUNK).transpose(1, 0, 2)


def _layer_weights(i, w_in, sg_ln_g, sg_ln_b, w_s, b_s, w_pa, w_pb, w_o, ln1_g, ln1_b,
                   w_up, conv_w, conv_b, w_down, ln2_g, ln2_b):
    row = lambda a: a.reshape(1, -1)
    bs = jnp.repeat(b_s[i].reshape(SG_GROUPS // 2, 2, CHUNK).transpose(0, 2, 1),
                    SG_WIDTH // SG_GROUPS, axis=-1)
    mix = (w_s[i].astype(BF16), bs, w_pa[i].astype(BF16), w_pb[i].astype(BF16),
           w_o[i].astype(BF16), row(ln1_g[i]), row(ln1_b[i]))
    cw, cb = conv_w[i], conv_b[i]
    cp = jnp.concatenate([cw[:, :D_FF], cb[None, :D_FF], cw[:, D_FF:], cb[None, D_FF:]], axis=0)
    cp = cp.reshape(8, N_FF_CHUNKS, FF_CHUNK).transpose(1, 0, 2)
    ffn = (_chunk_cols(w_up[i][:, :D_FF]).astype(BF16), _chunk_cols(w_up[i][:, D_FF:]).astype(BF16),
           cp, w_down[i].astype(BF16).reshape(N_FF_CHUNKS, FF_CHUNK, D_MODEL),
           row(ln2_g[i]), row(ln2_b[i]))
    return w_in[i].astype(BF16), row(sg_ln_g[i]), row(sg_ln_b[i]), mix, ffn


def kernel(x, c, ctx, c_ctx, w_ada, b_ada, w_in, rpb, sg_ln_g, sg_ln_b, w_s, b_s, w_pa, w_pb, w_o,
           ln1_g, ln1_b, w_up, conv_w, conv_b, w_down, ln2_g, ln2_b):
    assert x.shape == (BATCH, SEQ, D_MODEL) and ctx.shape == (BATCH, CTX_LEN, D_MODEL)
    cc = jnp.concatenate([c, c_ctx[None, :], jnp.zeros((MOD_ROWS - BATCH - 1, D_MODEL), F32)], axis=0)
    mods = _modulation(cc, w_ada, b_ada)
    rope = _rope_tables()
    tabs = _bias_tables(rpb)
    xl = x.reshape(BATCH * SEQ, D_MODEL)
    xc = ctx.reshape(BATCH * CTX_LEN, D_MODEL)
    for i in range(DEPTH):
        w_in_i, lng, lnb, mix_w, ffn_w = _layer_weights(
            i, w_in, sg_ln_g, sg_ln_b, w_s, b_s, w_pa, w_pb, w_o, ln1_g, ln1_b,
            w_up, conv_w, conv_b, w_down, ln2_g, ln2_b)
        mod = mods[i]
        tab = tabs[i]
        if i < DEPTH - 1:
            c_acts = _in_projection(xc, mod, w_in_i, lng, lnb, None, mode="ctx", tm=CTX_LEN, seq=CTX_LEN)
            ctx_kv = (c_acts[1], c_acts[2])
        else:
            ctx_kv = _in_projection(xc, mod, w_in_i[:, NA_WIDTH:3 * NA_WIDTH], None, None, None,
                                    mode="ctx_kv", tm=CTX_LEN, seq=CTX_LEN)
        acts = _in_projection(xl, mod, w_in_i, lng, lnb, rope, mode="latent", tm=512, seq=SEQ)
        xl = _mix(xl, mod, acts, ctx_kv, tab, mix_w, latent=True)
        xl = _conv_ffn(xl, mod, ffn_w, latent=True, tm=512, seq=SEQ)
        if i < DEPTH - 1:
            xc = _mix(xc, mod, c_acts, None, None, mix_w, latent=False)
            xc = _conv_ffn(xc, mod, ffn_w, latent=False, tm=CTX_LEN, seq=CTX_LEN)
    return xl.reshape(BATCH, SEQ, D_MODEL)
```

```python
import functools

import jax
import jax.numpy as jnp
from jax import lax
from jax.experimental import pallas as pl
from jax.experimental.pallas import tpu as pltpu

D_MODEL = 1024
BATCH = 4
SEQ = 4096
DEPTH = 2
GRID_W = 64
CTX_LEN = 256
NA_HEADS = 8
HEAD_DIM = 64
NA_WIDTH = NA_HEADS * HEAD_DIM
WIN_H = 8
WIN_W = 16
ROPE_BASE = 10000.0
SG_GROUPS = 8
SG_WIDTH = 512
CHUNK = 128
D_FF = 2816
N_IN = 3 * NA_WIDTH + 2 * SG_WIDTH + 2 * D_MODEL
ALPHA = (2 * DEPTH) ** 0.25
LN_EPS = 1e-5
NEG_INF = -1e30

F32 = jnp.float32
BF16 = jnp.bfloat16

LANES = 128
HEAD_PAIRS = NA_HEADS // 2
MOD_ROWS = 8
CTX_MOD_ROW = BATCH
Q_ROWS = 4
Q_BLOCK = Q_ROWS * GRID_W
K_ROWS = 3 * Q_ROWS
TAB_LOW, TAB_LOW_FULL, TAB_MID, TAB_HIGH, TAB_HIGH_FULL, TAB_MASKED = range(6)
TAB_KEY_BLOCK = (0, 0, 1, 2, 2, 0)
N_TAB_KINDS = len(TAB_KEY_BLOCK)
FF_CHUNK = 256
N_FF_CHUNKS = D_FF // FF_CHUNK
HALO = 8
VMEM_LIMIT = 56 * 1024 * 1024


def _dot(a, b):
    return jnp.dot(a, b, preferred_element_type=F32)


def _dot_nt(a, b):
    return lax.dot_general(a, b, (((1,), (1,)), ((), ())), preferred_element_type=F32)


def _layer_norm(v, g, b):
    mu = jnp.mean(v, axis=-1, keepdims=True)
    d = v - mu
    var = jnp.mean(d * d, axis=-1, keepdims=True)
    return d * lax.rsqrt(var + LN_EPS) * g + b


def _params(n_axes, flags=None):
    return pltpu.CompilerParams(dimension_semantics=("arbitrary",) * n_axes,
                                vmem_limit_bytes=VMEM_LIMIT, flags=flags)


def _resident(shape):
    zeros = (0,) * len(shape)
    return pl.BlockSpec(shape, lambda *_: zeros, pipeline_mode=pl.Buffered(1))


def _mod_kernel(cc_ref, w_ref, b_ref, o_ref):
    s = jax.nn.silu(cc_ref[...]).astype(BF16)
    o_ref[...] = _dot(s, w_ref[...].astype(BF16)) + b_ref[...]


def _modulation(cc, w_ada, b_ada):
    tn = 1536
    n6 = 6 * D_MODEL
    out = pl.pallas_call(
        _mod_kernel,
        grid=(DEPTH, n6 // tn),
        in_specs=[pl.BlockSpec((MOD_ROWS, D_MODEL), lambda l, n: (0, 0)),
                  pl.BlockSpec((None, D_MODEL, tn), lambda l, n: (l, 0, n)),
                  pl.BlockSpec((None, 1, tn), lambda l, n: (l, 0, n))],
        out_specs=pl.BlockSpec((None, MOD_ROWS, tn), lambda l, n: (l, 0, n)),
        out_shape=jax.ShapeDtypeStruct((DEPTH, MOD_ROWS, n6), F32),
        compiler_params=_params(2),
        name="adaln_modulation",
    )(cc, w_ada, b_ada.reshape(DEPTH, 1, n6))
    return out.reshape(DEPTH, MOD_ROWS, 6, D_MODEL)


def _rope_store(z, cos, sin, first_half, out_ref, scale):
    for j in range(NA_WIDTH // LANES):
        zj = z[:, j * LANES:(j + 1) * LANES]
        partner = jnp.where(first_half, pltpu.roll(zj, LANES - 16, 1), pltpu.roll(zj, 16, 1))
        r = zj * cos + partner * sin
        if scale != 1.0:
            r = r * scale
        out_ref[:, j * LANES:(j + 1) * LANES] = r.astype(out_ref.dtype)


def _inproj_kernel(*refs, mode):
    if mode == "latent":
        (x_ref, mod_ref, w_ref, lng_ref, lnb_ref, cos_ref, sin_ref,
         qr_ref, qp_ref, kr_ref, v_ref, gu_ref, svn_ref, sga_ref, sgb_ref) = refs
    elif mode == "ctx":
        (x_ref, mod_ref, w_ref, lng_ref, lnb_ref,
         qp_ref, kr_ref, v_ref, gu_ref, svn_ref, sga_ref, sgb_ref) = refs
    else:
        x_ref, mod_ref, w_ref, kr_ref, v_ref = refs

    scale = HEAD_DIM ** -0.5
    h = (x_ref[...] * (1.0 + mod_ref[1:2, :]) + mod_ref[0:1, :]).astype(BF16)

    def proj(lo, width):
        return _dot(h, w_ref[:, lo:lo + width])

    if mode == "ctx_kv":
        kr_ref[...] = proj(0, NA_WIDTH).astype(BF16)
        v_ref[...] = proj(NA_WIDTH, NA_WIDTH).astype(BF16)
        return

    zq = proj(0, NA_WIDTH)
    zk = proj(NA_WIDTH, NA_WIDTH)
    qp_ref[...] = (zq * scale).astype(BF16)
    if mode == "latent":
        cos = cos_ref[...]
        sin = sin_ref[...]
        lane = lax.broadcasted_iota(jnp.int32, cos.shape, 1)
        first_half = (lane % 32) < 16
        _rope_store(zq, cos, sin, first_half, qr_ref, scale)
        _rope_store(zk, cos, sin, first_half, kr_ref, 1.0)
    else:
        kr_ref[...] = zk.astype(BF16)
    v_ref[...] = proj(2 * NA_WIDTH, NA_WIDTH).astype(BF16)
    gu_ref[...] = jax.nn.gelu(proj(3 * NA_WIDTH, SG_WIDTH)).astype(BF16)
    sv = jax.nn.gelu(proj(3 * NA_WIDTH + SG_WIDTH, SG_WIDTH))
    svn_ref[...] = _layer_norm(sv, lng_ref[...], lnb_ref[...]).astype(BF16)
    lo = 3 * NA_WIDTH + 2 * SG_WIDTH
    sga_ref[...] = jax.nn.sigmoid(proj(lo, D_MODEL)).astype(BF16)
    sgb_ref[...] = jax.nn.sigmoid(proj(lo + D_MODEL, D_MODEL)).astype(BF16)


def _in_projection(x2d, mod, w, lng, lnb, rope, *, mode, tm, seq):
    t = x2d.shape[0]
    tiles_per_seq = seq // tm
    if mode == "latent":
        mod_map = lambda i: (i // tiles_per_seq, 0, 0)
    else:
        mod_map = lambda i: (CTX_MOD_ROW, 0, 0)
    tok = lambda width: pl.BlockSpec((tm, width), lambda i: (i, 0))
    in_specs = [tok(D_MODEL), pl.BlockSpec((None, 6, D_MODEL), mod_map), _resident(w.shape)]
    args = [x2d, mod, w]
    if mode != "ctx_kv":
        in_specs += [_resident((1, SG_WIDTH)), _resident((1, SG_WIDTH))]
        args += [lng, lnb]
    if mode == "latent":
        pos = pl.BlockSpec((tm, LANES), lambda i: (i % tiles_per_seq, 0))
        in_specs += [pos, pos]
        args += list(rope)
    widths = {"latent": [NA_WIDTH] * 6 + [D_MODEL] * 2,
              "ctx": [NA_WIDTH] * 5 + [D_MODEL] * 2,
              "ctx_kv": [NA_WIDTH] * 2}[mode]
    return pl.pallas_call(
        functools.partial(_inproj_kernel, mode=mode),
        grid=(t // tm,),
        in_specs=in_specs,
        out_specs=[tok(wd) for wd in widths],
        out_shape=[jax.ShapeDtypeStruct((t, wd), BF16) for wd in widths],
        compiler_params=_params(1),
        name="in_projection_" + mode,
    )(*args)


def _mix_kernel(*refs, latent, n_sub):
    n_nbr = n_sub + 2
    if latent:
        x_ref, mod_ref, qr_ref, qp_ref = refs[:4]
        k_refs = refs[4:4 + n_nbr]
        v_refs = refs[4 + n_nbr:4 + 2 * n_nbr]
        (kc_ref, vc_ref, tab_ref, gu_ref, svn_ref, sga_ref, sgb_ref, ws_ref, bs_ref,
         wpa_ref, wpb_ref, wo_ref, lng_ref, lnb_ref, o_ref, oa_ref, ob_ref) = refs[4 + 2 * n_nbr:]
    else:
        (x_ref, mod_ref, qp_ref, kc_ref, vc_ref, gu_ref, svn_ref, sga_ref, sgb_ref,
         ws_ref, bs_ref, wpa_ref, wpb_ref, wo_ref, lng_ref, lnb_ref,
         o_ref, oa_ref, ob_ref) = refs

    lane = lax.broadcasted_iota(jnp.int32, (Q_BLOCK, LANES), 1)
    low_half = lane < HEAD_DIM
    first_step = pl.program_id(1) == 0
    last_step = pl.program_id(1) == pl.num_programs(1) - 1

    for sub in range(n_sub):
        rows = slice(sub * Q_BLOCK, (sub + 1) * Q_BLOCK)
        tabs = [TAB_LOW, TAB_MID, TAB_HIGH]
        if latent and sub == 0:
            tabs[0] = jnp.where(first_step, TAB_MASKED, tabs[0])
            tabs[2] = jnp.where(first_step, TAB_HIGH_FULL, tabs[2])
        if latent and sub == n_sub - 1:
            tabs[0] = jnp.where(last_step, TAB_LOW_FULL, tabs[0])
            tabs[2] = jnp.where(last_step, TAB_MASKED, tabs[2])
        for p in range(HEAD_PAIRS):
            cols = slice(p * LANES, (p + 1) * LANES)

            def per_head(q):
                return jnp.concatenate([jnp.where(low_half, q, 0), jnp.where(low_half, 0, q)], axis=0)

            s_parts = [_dot_nt(per_head(qp_ref[rows, cols]), kc_ref[:, cols])]
            if latent:
                q_lat = per_head(qr_ref[rows, cols])
                for m in range(3):
                    s = _dot_nt(q_lat, k_refs[sub + m][:, cols])
                    tab = tab_ref[tabs[m], 2 * p:2 * p + 2].reshape(2 * Q_BLOCK, Q_BLOCK)
                    s_parts.append(s + tab)
            mx = functools.reduce(jnp.maximum, s_parts).max(axis=-1, keepdims=True)
            p_parts = [jnp.exp(s - mx) for s in s_parts]
            denom = functools.reduce(jnp.add, p_parts).sum(axis=-1, keepdims=True)
            acc = _dot(p_parts[0].astype(BF16), vc_ref[:, cols])
            for m in range(len(p_parts) - 1):
                acc = acc + _dot(p_parts[m + 1].astype(BF16), v_refs[sub + m][:, cols])
            o = acc / denom
            oa_ref[rows, cols] = jnp.where(low_half, o[:Q_BLOCK], o[Q_BLOCK:]).astype(BF16)

    half = lax.broadcasted_iota(jnp.int32, (CHUNK, LANES), 1) < (SG_WIDTH // SG_GROUPS)
    for c2 in range(n_sub * Q_BLOCK // (2 * CHUNK)):
        chunk_rows = [slice((2 * c2 + i) * CHUNK, (2 * c2 + i + 1) * CHUNK) for i in range(2)]
        for p in range(SG_GROUPS // 2):
            cols = slice(p * LANES, (p + 1) * LANES)
            vl = jnp.concatenate([svn_ref[r, cols] for r in chunk_rows], axis=1)
            both = _dot(ws_ref[p], vl)
            for i, r in enumerate(chunk_rows):
                lanes = slice(i * LANES, (i + 1) * LANES)
                mixed = jnp.where(half, both[:CHUNK, lanes], both[CHUNK:, lanes]) + bs_ref[p]
                ob_ref[r, cols] = (gu_ref[r, cols].astype(F32) * mixed).astype(BF16)

    ya = _dot(oa_ref[...], wpa_ref[...])
    yb = _dot(ob_ref[...], wpb_ref[...])
    y = sga_ref[...].astype(F32) * ya + sgb_ref[...].astype(F32) * yb
    out = _dot(y.astype(BF16), wo_ref[...])
    r = ALPHA * x_ref[...] + mod_ref[2:3, :] * out
    o_ref[...] = _layer_norm(r, lng_ref[...], lnb_ref[...])


def _mix(x2d, mod, acts, ctx_kv, tab, wts, *, latent, n_sub):
    t = x2d.shape[0]
    seq = SEQ if latent else CTX_LEN
    tq = n_sub * Q_BLOCK
    steps = seq // tq
    blocks_per_seq = seq // Q_BLOCK
    tok = lambda width: pl.BlockSpec((tq, width), lambda b, j: (b * steps + j, 0))
    ctx_spec = pl.BlockSpec((CTX_LEN, NA_WIDTH), lambda b, j: (b, 0))
    w_specs = [_resident(a.shape) for a in wts]
    if latent:
        qr, qp, kr, v, gu, svn, sga, sgb = acts
        kc, vc = ctx_kv
        mod_map = lambda b, j: (b, 0, 0)

        def nbr(m):
            return pl.BlockSpec(
                (Q_BLOCK, NA_WIDTH),
                lambda b, j: (b * blocks_per_seq
                              + jnp.clip(n_sub * j - 1 + m, 0, blocks_per_seq - 1), 0))

        nbrs = [nbr(m) for m in range(n_sub + 2)]
        in_specs = ([tok(D_MODEL), pl.BlockSpec((None, 6, D_MODEL), mod_map),
                     tok(NA_WIDTH), tok(NA_WIDTH)] + nbrs + nbrs
                    + [ctx_spec, ctx_spec, _resident(tab.shape)]
                    + [tok(NA_WIDTH), tok(NA_WIDTH), tok(D_MODEL), tok(D_MODEL)] + w_specs)
        args = ([x2d, mod, qr, qp] + [kr] * (n_sub + 2) + [v] * (n_sub + 2)
                + [kc, vc, tab, gu, svn, sga, sgb] + list(wts))
    else:
        qp, kc, vc, gu, svn, sga, sgb = acts
        mod_map = lambda b, j: (CTX_MOD_ROW, 0, 0)
        in_specs = ([tok(D_MODEL), pl.BlockSpec((None, 6, D_MODEL), mod_map),
                     tok(NA_WIDTH), ctx_spec, ctx_spec]
                    + [tok(NA_WIDTH), tok(NA_WIDTH), tok(D_MODEL), tok(D_MODEL)] + w_specs)
        args = [x2d, mod, qp, kc, vc, gu, svn, sga, sgb] + list(wts)
    return pl.pallas_call(
        functools.partial(_mix_kernel, latent=latent, n_sub=n_sub),
        grid=(BATCH, steps),
        in_specs=in_specs,
        out_specs=tok(D_MODEL),
        out_shape=jax.ShapeDtypeStruct((t, D_MODEL), F32),
        scratch_shapes=[pltpu.VMEM((tq, NA_WIDTH), BF16),
                        pltpu.VMEM((tq, SG_WIDTH), BF16)],
        compiler_params=_params(2),
        name="mix_latent" if latent else "mix_ctx",
    )(*args)


def _ffn_kernel(x_ref, prev_ref, next_ref, mod_ref, wu_ref, cw_ref, cb_ref, wd_ref,
                lng_ref, lnb_ref, o_ref, h_ref, act_ref, f_ref, *z_refs, tm, tiles_per_seq):
    i = pl.program_id(0)
    has_prev = (i % tiles_per_seq != 0).astype(F32)
    has_next = (i % tiles_per_seq != tiles_per_seq - 1).astype(F32)
    shift = mod_ref[3:4, :]
    scale1 = 1.0 + mod_ref[4:5, :]
    rows = tm + 2 * HALO
    h_ref[0:tm, :] = (x_ref[...] * scale1 + shift).astype(BF16)
    h_ref[tm:rows, :] = jnp.concatenate(
        [(prev_ref[...] * scale1 + shift) * has_prev,
         (next_ref[...] * scale1 + shift) * has_next], axis=0).astype(BF16)

    def conv(lo, z_ref):
        z = _dot(h_ref[...], wu_ref[:, lo:lo + FF_CHUNK])
        outs = []
        for t in range(FF_CHUNK // LANES):
            cols = slice(lo + t * LANES, lo + (t + 1) * LANES)
            zt = z[:, t * LANES:(t + 1) * LANES]
            z_ref[t, 0:HALO, :] = zt[tm:tm + HALO]
            z_ref[t, HALO:HALO + tm, :] = zt[0:tm]
            z_ref[t, HALO + tm:rows, :] = zt[tm + HALO:rows]
            outs.append(cw_ref[0:1, cols] * z_ref[t, HALO - 1:HALO - 1 + tm, :]
                        + cw_ref[1:2, cols] * zt[0:tm]
                        + cw_ref[2:3, cols] * z_ref[t, HALO + 1:HALO + 1 + tm, :]
                        + cb_ref[0:1, cols])
        return jnp.concatenate(outs, axis=1)

    def down(lo, width):
        return _dot(act_ref[:, lo:lo + width], wd_ref[lo:lo + width, :])

    assert N_FF_CHUNKS % 2 == 1
    for c in range(N_FF_CHUNKS):
        lo = c * FF_CHUNK
        zg_ref, za_ref = z_refs[2 * (c % 2)], z_refs[2 * (c % 2) + 1]
        act_ref[:, lo:lo + FF_CHUNK] = (jax.nn.silu(conv(D_FF + lo, zg_ref))
                                        * conv(lo, za_ref)).astype(BF16)
        if c == 1:
            f_ref[...] = down(0, 2 * FF_CHUNK)
        elif c % 2 == 1:
            f_ref[...] += down(lo - FF_CHUNK, 2 * FF_CHUNK)
    f = f_ref[...] + down(D_FF - FF_CHUNK, FF_CHUNK)
    r = ALPHA * x_ref[...] + mod_ref[5:6, :] * f
    o_ref[...] = _layer_norm(r, lng_ref[...], lnb_ref[...])


def _conv_ffn(x2d, mod, wts, *, latent, tm, seq):
    t = x2d.shape[0]
    tiles_per_seq = seq // tm
    n_halo_blocks = t // HALO
    per_tile = tm // HALO
    if latent:
        mod_map = lambda i: (i // tiles_per_seq, 0, 0)
    else:
        mod_map = lambda i: (CTX_MOD_ROW, 0, 0)
    tok = pl.BlockSpec((tm, D_MODEL), lambda i: (i, 0))
    prev = pl.BlockSpec((HALO, D_MODEL), lambda i: (jnp.maximum(i * per_tile - 1, 0), 0))
    nxt = pl.BlockSpec((HALO, D_MODEL),
                       lambda i: (jnp.minimum((i + 1) * per_tile, n_halo_blocks - 1), 0))
    return pl.pallas_call(
        functools.partial(_ffn_kernel, tm=tm, tiles_per_seq=tiles_per_seq),
        grid=(t // tm,),
        in_specs=[tok, prev, nxt, pl.BlockSpec((None, 6, D_MODEL), mod_map)]
                 + [_resident(a.shape) for a in wts],
        out_specs=tok,
        out_shape=jax.ShapeDtypeStruct((t, D_MODEL), F32),
        scratch_shapes=[pltpu.VMEM((tm + 2 * HALO, D_MODEL), BF16),
                        pltpu.VMEM((tm, D_FF), BF16),
                        pltpu.VMEM((tm, D_MODEL), F32)]
                       + [pltpu.VMEM((FF_CHUNK // LANES, tm + 2 * HALO, LANES), F32)] * 4,
        compiler_params=_params(1),
        name="conv_ffn_latent" if latent else "conv_ffn_ctx",
    )(x2d, x2d, x2d, mod, *wts)


def _rope_tables():
    half = HEAD_DIM // 2
    nf = half // 2
    inv = ROPE_BASE ** (-jnp.arange(nf, dtype=F32) / nf)
    t = jnp.arange(SEQ)

    def part(pos):
        ang = pos.astype(F32)[:, None] * inv[None, :]
        cos = jnp.cos(ang)
        sin = jnp.sin(ang)
        return jnp.concatenate([cos, cos], axis=-1), jnp.concatenate([-sin, sin], axis=-1)

    cr, sr = part(t // GRID_W)
    cc, sc = part(t % GRID_W)
    cos = jnp.concatenate([cr, cc], axis=-1)
    sin = jnp.concatenate([sr, sc], axis=-1)
    reps = LANES // HEAD_DIM
    return jnp.tile(cos, (1, reps)), jnp.tile(sin, (1, reps))


def _row_ok(kind, qi, ki):
    if kind == TAB_MASKED:
        return False
    if kind in (TAB_LOW_FULL, TAB_MID, TAB_HIGH_FULL):
        return True
    return qi <= ki < qi + WIN_H


def _bias_kernel(rpb_ref, o_ref, c_ref):
    n_dy, n_dx = 2 * WIN_H - 1, 2 * WIN_W - 1
    head = pl.program_id(0) * NA_HEADS + pl.program_id(1)
    qc = lax.broadcasted_iota(jnp.int32, (GRID_W, GRID_W), 0)
    kc = lax.broadcasted_iota(jnp.int32, (GRID_W, GRID_W), 1)
    win_start = jnp.clip(qc - WIN_W // 2, 0, GRID_W - WIN_W)
    col_ok = (kc >= win_start) & (kc < win_start + WIN_W)
    dx = kc - qc + WIN_W - 1
    masked = jnp.full((GRID_W, GRID_W), NEG_INF, F32)

    def toeplitz(dy, carry):
        acc = masked
        for b in range(n_dx):
            acc = jnp.where(col_ok & (dx == b), rpb_ref[(head * n_dy + dy) * n_dx + b], acc)
        c_ref[dy] = acc
        return carry

    lax.fori_loop(0, n_dy, toeplitz, 0)
    for kind in range(N_TAB_KINDS):
        key_block = TAB_KEY_BLOCK[kind]
        for qi in range(Q_ROWS):
            for kp in range(Q_ROWS // 2):
                pair = []
                for kl in (2 * kp, 2 * kp + 1):
                    ki = Q_ROWS * key_block + kl
                    pair.append(c_ref[ki - qi + WIN_H // 2 - 1] if _row_ok(kind, qi, ki)
                                else masked)
                o_ref[kind, qi * GRID_W:(qi + 1) * GRID_W, kp * LANES:(kp + 1) * LANES] = (
                    jnp.concatenate(pair, axis=1))


def _bias_tables(rpb):
    return pl.pallas_call(
        _bias_kernel,
        grid=(DEPTH, NA_HEADS),
        in_specs=[pl.BlockSpec(memory_space=pltpu.SMEM)],
        out_specs=pl.BlockSpec((None, N_TAB_KINDS, None, Q_BLOCK, Q_BLOCK),
                               lambda l, h: (l, 0, h, 0, 0)),
        out_shape=jax.ShapeDtypeStruct((DEPTH, N_TAB_KINDS, NA_HEADS, Q_BLOCK, Q_BLOCK), F32),
        scratch_shapes=[pltpu.VMEM((2 * WIN_H - 1, GRID_W, GRID_W), F32)],
        compiler_params=_params(2),
        name="bias_tables",
    )(rpb.reshape(-1))


def _layer_weights(i, w_in, sg_ln_g, sg_ln_b, w_s, b_s, w_pa, w_pb, w_o, ln1_g, ln1_b,
                   w_up, conv_w, conv_b, w_down, ln2_g, ln2_b):
    row = lambda a: a.reshape(1, -1)
    bs = jnp.repeat(b_s[i].reshape(SG_GROUPS // 2, 2, CHUNK).transpose(0, 2, 1),
                    SG_WIDTH // SG_GROUPS, axis=-1)
    ws = w_s[i].astype(BF16).reshape(SG_GROUPS // 2, 2 * CHUNK, CHUNK)
    mix = (ws, bs, w_pa[i].astype(BF16), w_pb[i].astype(BF16),
           w_o[i].astype(BF16), row(ln1_g[i]), row(ln1_b[i]))
    ffn = (w_up[i].astype(BF16), conv_w[i], row(conv_b[i]), w_down[i].astype(BF16),
           row(ln2_g[i]), row(ln2_b[i]))
    return w_in[i].astype(BF16), row(sg_ln_g[i]), row(sg_ln_b[i]), mix, ffn


def kernel(x, c, ctx, c_ctx, w_ada, b_ada, w_in, rpb, sg_ln_g, sg_ln_b, w_s, b_s, w_pa, w_pb, w_o,
           ln1_g, ln1_b, w_up, conv_w, conv_b, w_down, ln2_g, ln2_b):
    assert x.shape == (BATCH, SEQ, D_MODEL) and ctx.shape == (BATCH, CTX_LEN, D_MODEL)
    cc = jnp.concatenate([c, c_ctx[None, :], jnp.zeros((MOD_ROWS - BATCH - 1, D_MODEL), F32)], axis=0)
    mods = _modulation(cc, w_ada, b_ada)
    rope = _rope_tables()
    tabs = _bias_tables(rpb)
    xl = x.reshape(BATCH * SEQ, D_MODEL)
    xc = ctx.reshape(BATCH * CTX_LEN, D_MODEL)
    for i in range(DEPTH):
        w_in_i, lng, lnb, mix_w, ffn_w = _layer_weights(
            i, w_in, sg_ln_g, sg_ln_b, w_s, b_s, w_pa, w_pb, w_o, ln1_g, ln1_b,
            w_up, conv_w, conv_b, w_down, ln2_g, ln2_b)
        mod = mods[i]
        tab = tabs[i]
        if i < DEPTH - 1:
            c_acts = _in_projection(xc, mod, w_in_i, lng, lnb, None, mode="ctx", tm=CTX_LEN, seq=CTX_LEN)
            ctx_kv = (c_acts[1], c_acts[2])
        else:
            ctx_kv = _in_projection(xc, mod, w_in_i[:, NA_WIDTH:3 * NA_WIDTH], None, None, None,
                                    mode="ctx_kv", tm=CTX_LEN, seq=CTX_LEN)
        acts = _in_projection(xl, mod, w_in_i, lng, lnb, rope, mode="latent", tm=512, seq=SEQ)
        xl = _mix(xl, mod, acts, ctx_kv, tab, mix_w, latent=True, n_sub=2)
        xl = _conv_ffn(xl, mod, ffn_w, latent=True, tm=512, seq=SEQ)
        if i < DEPTH - 1:
            xc = _mix(xc, mod, c_acts, None, None, mix_w, latent=False, n_sub=1)
            xc = _conv_ffn(xc, mod, ffn_w, latent=False, tm=CTX_LEN, seq=CTX_LEN)
    return xl.reshape(BATCH, SEQ, D_MODEL)
```

```python
import functools

import jax
import jax.numpy as jnp
from jax import lax
from jax.experimental import pallas as pl
from jax.experimental.pallas import tpu as pltpu

D_MODEL = 1024
BATCH = 4
SEQ = 4096
DEPTH = 2
GRID_W = 64
CTX_LEN = 256
NA_HEADS = 8
HEAD_DIM = 64
NA_WIDTH = NA_HEADS * HEAD_DIM
WIN_H = 8
WIN_W = 16
ROPE_BASE = 10000.0
SG_GROUPS = 8
SG_WIDTH = 512
CHUNK = 128
D_FF = 2816
N_IN = 3 * NA_WIDTH + 2 * SG_WIDTH + 2 * D_MODEL
ALPHA = (2 * DEPTH) ** 0.25
LN_EPS = 1e-5
NEG_INF = -1e30

F32 = jnp.float32
BF16 = jnp.bfloat16

LANES = 128
HEAD_PAIRS = NA_HEADS // 2
MOD_ROWS = 8
CTX_MOD_ROW = BATCH
Q_ROWS = 4
Q_BLOCK = Q_ROWS * GRID_W
K_ROWS = 3 * Q_ROWS
TAB_LOW, TAB_LOW_FULL, TAB_MID, TAB_HIGH, TAB_HIGH_FULL, TAB_MASKED = range(6)
TAB_KEY_BLOCK = (0, 0, 1, 2, 2, 0)
N_TAB_KINDS = len(TAB_KEY_BLOCK)
FF_CHUNK = 256
N_FF_CHUNKS = D_FF // FF_CHUNK
HALO = 8
VMEM_LIMIT = 56 * 1024 * 1024


def _dot(a, b):
    return jnp.dot(a, b, preferred_element_type=F32)


def _dot_nt(a, b):
    return lax.dot_general(a, b, (((1,), (1,)), ((), ())), preferred_element_type=F32)


def _layer_norm(v, g, b):
    mu = jnp.mean(v, axis=-1, keepdims=True)
    d = v - mu
    var = jnp.mean(d * d, axis=-1, keepdims=True)
    return d * lax.rsqrt(var + LN_EPS) * g + b


def _params(n_axes, flags=None):
    return pltpu.CompilerParams(dimension_semantics=("arbitrary",) * n_axes,
                                vmem_limit_bytes=VMEM_LIMIT, flags=flags)


def _resident(shape):
    zeros = (0,) * len(shape)
    return pl.BlockSpec(shape, lambda *_: zeros, pipeline_mode=pl.Buffered(1))


def _mod_kernel(cc_ref, w_ref, b_ref, o_ref):
    s = jax.nn.silu(cc_ref[...]).astype(BF16)
    o_ref[...] = _dot(s, w_ref[...].astype(BF16)) + b_ref[...]


def _modulation(cc, w_ada, b_ada):
    tn = 1536
    n6 = 6 * D_MODEL
    out = pl.pallas_call(
        _mod_kernel,
        grid=(DEPTH, n6 // tn),
        in_specs=[pl.BlockSpec((MOD_ROWS, D_MODEL), lambda l, n: (0, 0)),
                  pl.BlockSpec((None, D_MODEL, tn), lambda l, n: (l, 0, n)),
                  pl.BlockSpec((None, 1, tn), lambda l, n: (l, 0, n))],
        out_specs=pl.BlockSpec((None, MOD_ROWS, tn), lambda l, n: (l, 0, n)),
        out_shape=jax.ShapeDtypeStruct((DEPTH, MOD_ROWS, n6), F32),
        compiler_params=_params(2),
        name="adaln_modulation",
    )(cc, w_ada, b_ada.reshape(DEPTH, 1, n6))
    return out.reshape(DEPTH, MOD_ROWS, 6, D_MODEL)


def _rope_store(z, cos, sin, first_half, out_ref, scale):
    for j in range(NA_WIDTH // LANES):
        zj = z[:, j * LANES:(j + 1) * LANES]
        partner = jnp.where(first_half, pltpu.roll(zj, LANES - 16, 1), pltpu.roll(zj, 16, 1))
        r = zj * cos + partner * sin
        if scale != 1.0:
            r = r * scale
        out_ref[:, j * LANES:(j + 1) * LANES] = r.astype(out_ref.dtype)


def _inproj_kernel(*refs, mode):
    if mode == "latent":
        (x_ref, mod_ref, w_ref, lng_ref, lnb_ref, cos_ref, sin_ref,
         qr_ref, qp_ref, kr_ref, v_ref, gu_ref, svn_ref, sga_ref, sgb_ref) = refs
    elif mode == "ctx":
        (x_ref, mod_ref, w_ref, lng_ref, lnb_ref,
         qp_ref, kr_ref, v_ref, gu_ref, svn_ref, sga_ref, sgb_ref) = refs
    else:
        x_ref, mod_ref, w_ref, kr_ref, v_ref = refs

    scale = HEAD_DIM ** -0.5
    h = (x_ref[...] * (1.0 + mod_ref[1:2, :]) + mod_ref[0:1, :]).astype(BF16)

    def proj(lo, width):
        return _dot(h, w_ref[:, lo:lo + width])

    if mode == "ctx_kv":
        kr_ref[...] = proj(0, NA_WIDTH).astype(BF16)
        v_ref[...] = proj(NA_WIDTH, NA_WIDTH).astype(BF16)
        return

    zq = proj(0, NA_WIDTH)
    zk = proj(NA_WIDTH, NA_WIDTH)
    qp_ref[...] = (zq * scale).astype(BF16)
    if mode == "latent":
        cos = cos_ref[...]
        sin = sin_ref[...]
        lane = lax.broadcasted_iota(jnp.int32, cos.shape, 1)
        first_half = (lane % 32) < 16
        _rope_store(zq, cos, sin, first_half, qr_ref, scale)
        _rope_store(zk, cos, sin, first_half, kr_ref, 1.0)
    else:
        kr_ref[...] = zk.astype(BF16)
    v_ref[...] = proj(2 * NA_WIDTH, NA_WIDTH).astype(BF16)
    gu_ref[...] = jax.nn.gelu(proj(3 * NA_WIDTH, SG_WIDTH)).astype(BF16)
    sv = jax.nn.gelu(proj(3 * NA_WIDTH + SG_WIDTH, SG_WIDTH))
    svn_ref[...] = _layer_norm(sv, lng_ref[...], lnb_ref[...]).astype(BF16)
    lo = 3 * NA_WIDTH + 2 * SG_WIDTH
    sga_ref[...] = jax.nn.sigmoid(proj(lo, D_MODEL)).astype(BF16)
    sgb_ref[...] = jax.nn.sigmoid(proj(lo + D_MODEL, D_MODEL)).astype(BF16)


def _in_projection(x2d, mod, w, lng, lnb, rope, *, mode, tm, seq):
    t = x2d.shape[0]
    tiles_per_seq = seq // tm
    if mode == "latent":
        mod_map = lambda i: (i // tiles_per_seq, 0, 0)
    else:
        mod_map = lambda i: (CTX_MOD_ROW, 0, 0)
    tok = lambda width: pl.BlockSpec((tm, width), lambda i: (i, 0))
    in_specs = [tok(D_MODEL), pl.BlockSpec((None, 6, D_MODEL), mod_map), _resident(w.shape)]
    args = [x2d, mod, w]
    if mode != "ctx_kv":
        in_specs += [_resident((1, SG_WIDTH)), _resident((1, SG_WIDTH))]
        args += [lng, lnb]
    if mode == "latent":
        pos = pl.BlockSpec((tm, LANES), lambda i: (i % tiles_per_seq, 0))
        in_specs += [pos, pos]
        args += list(rope)
    widths = {"latent": [NA_WIDTH] * 6 + [D_MODEL] * 2,
              "ctx": [NA_WIDTH] * 5 + [D_MODEL] * 2,
              "ctx_kv": [NA_WIDTH] * 2}[mode]
    return pl.pallas_call(
        functools.partial(_inproj_kernel, mode=mode),
        grid=(t // tm,),
        in_specs=in_specs,
        out_specs=[tok(wd) for wd in widths],
        out_shape=[jax.ShapeDtypeStruct((t, wd), BF16) for wd in widths],
        compiler_params=_params(1),
        name="in_projection_" + mode,
    )(*args)


def _mix_kernel(*refs, latent, n_sub):
    n_nbr = n_sub + 2
    if latent:
        x_ref, mod_ref, qr_ref, qp_ref = refs[:4]
        k_refs = refs[4:4 + n_nbr]
        v_refs = refs[4 + n_nbr:4 + 2 * n_nbr]
        (kc_ref, vc_ref, tab_ref, gu_ref, svn_ref, sga_ref, sgb_ref, ws_ref, bs_ref,
         wpa_ref, wpb_ref, wo_ref, lng_ref, lnb_ref, o_ref, oa_ref, ob_ref) = refs[4 + 2 * n_nbr:]
    else:
        (x_ref, mod_ref, qp_ref, kc_ref, vc_ref, gu_ref, svn_ref, sga_ref, sgb_ref,
         ws_ref, bs_ref, wpa_ref, wpb_ref, wo_ref, lng_ref, lnb_ref,
         o_ref, oa_ref, ob_ref) = refs

    lane = lax.broadcasted_iota(jnp.int32, (Q_BLOCK, LANES), 1)
    low_half = lane < HEAD_DIM
    first_step = pl.program_id(1) == 0
    last_step = pl.program_id(1) == pl.num_programs(1) - 1

    for sub in range(n_sub):
        rows = slice(sub * Q_BLOCK, (sub + 1) * Q_BLOCK)
        tabs = [TAB_LOW, TAB_MID, TAB_HIGH]
        if latent and sub == 0:
            tabs[0] = jnp.where(first_step, TAB_MASKED, tabs[0])
            tabs[2] = jnp.where(first_step, TAB_HIGH_FULL, tabs[2])
        if latent and sub == n_sub - 1:
            tabs[0] = jnp.where(last_step, TAB_LOW_FULL, tabs[0])
            tabs[2] = jnp.where(last_step, TAB_MASKED, tabs[2])
        for p in range(HEAD_PAIRS):
            cols = slice(p * LANES, (p + 1) * LANES)

            def per_head(q):
                return jnp.concatenate([jnp.where(low_half, q, 0), jnp.where(low_half, 0, q)], axis=0)

            s_parts = [_dot_nt(per_head(qp_ref[rows, cols]), kc_ref[:, cols])]
            if latent:
                q_lat = per_head(qr_ref[rows, cols])
                for m in range(3):
                    s = _dot_nt(q_lat, k_refs[sub + m][:, cols])
                    tab = tab_ref[tabs[m], 2 * p:2 * p + 2].reshape(2 * Q_BLOCK, Q_BLOCK)
                    s_parts.append(s + tab)
            mx = functools.reduce(jnp.maximum, s_parts).max(axis=-1, keepdims=True)
            p_parts = [jnp.exp(s - mx) for s in s_parts]
            denom = functools.reduce(jnp.add, p_parts).sum(axis=-1, keepdims=True)
            acc = _dot(p_parts[0].astype(BF16), vc_ref[:, cols])
            for m in range(len(p_parts) - 1):
                acc = acc + _dot(p_parts[m + 1].astype(BF16), v_refs[sub + m][:, cols])
            o = acc / denom
            oa_ref[rows, cols] = jnp.where(low_half, o[:Q_BLOCK], o[Q_BLOCK:]).astype(BF16)

    half = lax.broadcasted_iota(jnp.int32, (CHUNK, LANES), 1) < (SG_WIDTH // SG_GROUPS)
    for c2 in range(n_sub * Q_BLOCK // (2 * CHUNK)):
        chunk_rows = [slice((2 * c2 + i) * CHUNK, (2 * c2 + i + 1) * CHUNK) for i in range(2)]
        for p in range(SG_GROUPS // 2):
            cols = slice(p * LANES, (p + 1) * LANES)
            vl = jnp.concatenate([svn_ref[r, cols] for r in chunk_rows], axis=1)
            both = _dot(ws_ref[p], vl)
            for i, r in enumerate(chunk_rows):
                lanes = slice(i * LANES, (i + 1) * LANES)
                mixed = jnp.where(half, both[:CHUNK, lanes], both[CHUNK:, lanes]) + bs_ref[p]
                ob_ref[r, cols] = (gu_ref[r, cols].astype(F32) * mixed).astype(BF16)

    for sub in range(n_sub):
        rows = slice(sub * Q_BLOCK, (sub + 1) * Q_BLOCK)
        ya = _dot(oa_ref[rows, :], wpa_ref[...])
        yb = _dot(ob_ref[rows, :], wpb_ref[...])
        y = sga_ref[rows, :].astype(F32) * ya + sgb_ref[rows, :].astype(F32) * yb
        out = _dot(y.astype(BF16), wo_ref[...])
        r = ALPHA * x_ref[rows, :] + mod_ref[2:3, :] * out
        o_ref[rows, :] = _layer_norm(r, lng_ref[...], lnb_ref[...])


def _mix(x2d, mod, acts, ctx_kv, tab, wts, *, latent, n_sub):
    t = x2d.shape[0]
    seq = SEQ if latent else CTX_LEN
    tq = n_sub * Q_BLOCK
    steps = seq // tq
    blocks_per_seq = seq // Q_BLOCK
    tok = lambda width: pl.BlockSpec((tq, width), lambda b, j: (b * steps + j, 0))
    ctx_spec = pl.BlockSpec((CTX_LEN, NA_WIDTH), lambda b, j: (b, 0))
    w_specs = [_resident(a.shape) for a in wts]
    if latent:
        qr, qp, kr, v, gu, svn, sga, sgb = acts
        kc, vc = ctx_kv
        mod_map = lambda b, j: (b, 0, 0)

        def nbr(m):
            return pl.BlockSpec(
                (Q_BLOCK, NA_WIDTH),
                lambda b, j: (b * blocks_per_seq
                              + jnp.clip(n_sub * j - 1 + m, 0, blocks_per_seq - 1), 0))

        nbrs = [nbr(m) for m in range(n_sub + 2)]
        in_specs = ([tok(D_MODEL), pl.BlockSpec((None, 6, D_MODEL), mod_map),
                     tok(NA_WIDTH), tok(NA_WIDTH)] + nbrs + nbrs
                    + [ctx_spec, ctx_spec, _resident(tab.shape)]
                    + [tok(NA_WIDTH), tok(NA_WIDTH), tok(D_MODEL), tok(D_MODEL)] + w_specs)
        args = ([x2d, mod, qr, qp] + [kr] * (n_sub + 2) + [v] * (n_sub + 2)
                + [kc, vc, tab, gu, svn, sga, sgb] + list(wts))
    else:
        qp, kc, vc, gu, svn, sga, sgb = acts
        mod_map = lambda b, j: (CTX_MOD_ROW, 0, 0)
        in_specs = ([tok(D_MODEL), pl.BlockSpec((None, 6, D_MODEL), mod_map),
                     tok(NA_WIDTH), ctx_spec, ctx_spec]
                    + [tok(NA_WIDTH), tok(NA_WIDTH), tok(D_MODEL), tok(D_MODEL)] + w_specs)
        args = [x2d, mod, qp, kc, vc, gu, svn, sga, sgb] + list(wts)
    return pl.pallas_call(
        functools.partial(_mix_kernel, latent=latent, n_sub=n_sub),
        grid=(BATCH, steps),
        in_specs=in_specs,
        out_specs=tok(D_MODEL),
        out_shape=jax.ShapeDtypeStruct((t, D_MODEL), F32),
        scratch_shapes=[pltpu.VMEM((tq, NA_WIDTH), BF16),
                        pltpu.VMEM((tq, SG_WIDTH), BF16)],
        compiler_params=_params(2),
        name="mix_latent" if latent else "mix_ctx",
    )(*args)


def _ffn_kernel(x_ref, prev_ref, next_ref, mod_ref, wu_ref, cw_ref, cb_ref, wd_ref,
                lng_ref, lnb_ref, o_ref, h_ref, act_ref, f_ref, *z_refs, tm, tiles_per_seq):
    i = pl.program_id(0)
    has_prev = (i % tiles_per_seq != 0).astype(F32)
    has_next = (i % tiles_per_seq != tiles_per_seq - 1).astype(F32)
    shift = mod_ref[3:4, :]
    scale1 = 1.0 + mod_ref[4:5, :]
    rows = tm + 2 * HALO
    h_ref[0:tm, :] = (x_ref[...] * scale1 + shift).astype(BF16)
    h_ref[tm:rows, :] = jnp.concatenate(
        [(prev_ref[...] * scale1 + shift) * has_prev,
         (next_ref[...] * scale1 + shift) * has_next], axis=0).astype(BF16)

    def conv(lo, z_ref):
        z = _dot(h_ref[...], wu_ref[:, lo:lo + FF_CHUNK])
        outs = []
        for t in range(FF_CHUNK // LANES):
            cols = slice(lo + t * LANES, lo + (t + 1) * LANES)
            zt = z[:, t * LANES:(t + 1) * LANES]
            z_ref[t, 0:HALO, :] = zt[tm:tm + HALO]
            z_ref[t, HALO:HALO + tm, :] = zt[0:tm]
            z_ref[t, HALO + tm:rows, :] = zt[tm + HALO:rows]
            outs.append(cw_ref[0:1, cols] * z_ref[t, HALO - 1:HALO - 1 + tm, :]
                        + cw_ref[1:2, cols] * zt[0:tm]
                        + cw_ref[2:3, cols] * z_ref[t, HALO + 1:HALO + 1 + tm, :]
                        + cb_ref[0:1, cols])
        return jnp.concatenate(outs, axis=1)

    def down(lo, width):
        return _dot(act_ref[:, lo:lo + width], wd_ref[lo:lo + width, :])

    assert N_FF_CHUNKS % 2 == 1
    for c in range(N_FF_CHUNKS):
        lo = c * FF_CHUNK
        zg_ref, za_ref = z_refs[2 * (c % 2)], z_refs[2 * (c % 2) + 1]
        act_ref[:, lo:lo + FF_CHUNK] = (jax.nn.silu(conv(D_FF + lo, zg_ref))
                                        * conv(lo, za_ref)).astype(BF16)
        if c == 1:
            f_ref[...] = down(0, 2 * FF_CHUNK)
        elif c % 2 == 1:
            f_ref[...] += down(lo - FF_CHUNK, 2 * FF_CHUNK)
    f = f_ref[...] + down(D_FF - FF_CHUNK, FF_CHUNK)
    r = ALPHA * x_ref[...] + mod_ref[5:6, :] * f
    o_ref[...] = _layer_norm(r, lng_ref[...], lnb_ref[...])


def _conv_ffn(x2d, mod, wts, *, latent, tm, seq):
    t = x2d.shape[0]
    tiles_per_seq = seq // tm
    n_halo_blocks = t // HALO
    per_tile = tm // HALO
    if latent:
        mod_map = lambda i: (i // tiles_per_seq, 0, 0)
    else:
        mod_map = lambda i: (CTX_MOD_ROW, 0, 0)
    tok = pl.BlockSpec((tm, D_MODEL), lambda i: (i, 0))
    prev = pl.BlockSpec((HALO, D_MODEL), lambda i: (jnp.maximum(i * per_tile - 1, 0), 0))
    nxt = pl.BlockSpec((HALO, D_MODEL),
                       lambda i: (jnp.minimum((i + 1) * per_tile, n_halo_blocks - 1), 0))
    return pl.pallas_call(
        functools.partial(_ffn_kernel, tm=tm, tiles_per_seq=tiles_per_seq),
        grid=(t // tm,),
        in_specs=[tok, prev, nxt, pl.BlockSpec((None, 6, D_MODEL), mod_map)]
                 + [_resident(a.shape) for a in wts],
        out_specs=tok,
        out_shape=jax.ShapeDtypeStruct((t, D_MODEL), F32),
        scratch_shapes=[pltpu.VMEM((tm + 2 * HALO, D_MODEL), BF16),
                        pltpu.VMEM((tm, D_FF), BF16),
                        pltpu.VMEM((tm, D_MODEL), F32)]
                       + [pltpu.VMEM((FF_CHUNK // LANES, tm + 2 * HALO, LANES), F32)] * 4,
        compiler_params=_params(1),
        name="conv_ffn_latent" if latent else "conv_ffn_ctx",
    )(x2d, x2d, x2d, mod, *wts)


def _rope_tables():
    half = HEAD_DIM // 2
    nf = half // 2
    inv = ROPE_BASE ** (-jnp.arange(nf, dtype=F32) / nf)
    t = jnp.arange(SEQ)

    def part(pos):
        ang = pos.astype(F32)[:, None] * inv[None, :]
        cos = jnp.cos(ang)
        sin = jnp.sin(ang)
        return jnp.concatenate([cos, cos], axis=-1), jnp.concatenate([-sin, sin], axis=-1)

    cr, sr = part(t // GRID_W)
    cc, sc = part(t % GRID_W)
    cos = jnp.concatenate([cr, cc], axis=-1)
    sin = jnp.concatenate([sr, sc], axis=-1)
    reps = LANES // HEAD_DIM
    return jnp.tile(cos, (1, reps)), jnp.tile(sin, (1, reps))


def _row_ok(kind, qi, ki):
    if kind == TAB_MASKED:
        return False
    if kind in (TAB_LOW_FULL, TAB_MID, TAB_HIGH_FULL):
        return True
    return qi <= ki < qi + WIN_H


BIAS_HI = 32


def _bias_kernel(rp_ref, o_ref, pt_ref):
    n_pairs = 2 * WIN_H - 2
    qc = lax.broadcasted_iota(jnp.int32, (GRID_W, LANES), 0)
    lane = lax.broadcasted_iota(jnp.int32, (GRID_W, LANES), 1)
    upper = lane >= GRID_W
    kc = jnp.where(upper, lane - GRID_W, lane)
    win_start = jnp.clip(qc - WIN_W // 2, 0, GRID_W - WIN_W)
    col_ok = (kc >= win_start) & (kc < win_start + WIN_W)
    idx = jnp.clip(kc - qc + WIN_W - 1, 0, 2 * WIN_W - 2) + jnp.where(upper, BIAS_HI, 0)
    masked = jnp.full((GRID_W, LANES), NEG_INF, F32)
    for dy in range(n_pairs):
        src = jnp.broadcast_to(rp_ref[dy:dy + 1, :], (GRID_W, LANES))
        pt_ref[dy] = jnp.where(col_ok, jnp.take_along_axis(src, idx, axis=1), NEG_INF)
    for kind in range(N_TAB_KINDS):
        for qi in range(Q_ROWS):
            for kp in range(Q_ROWS // 2):
                ki = Q_ROWS * TAB_KEY_BLOCK[kind] + 2 * kp
                ok = (_row_ok(kind, qi, ki), _row_ok(kind, qi, ki + 1))
                tile = pt_ref[ki - qi + WIN_H // 2 - 1] if any(ok) else masked
                if ok == (True, False):
                    tile = jnp.where(upper, NEG_INF, tile)
                elif ok == (False, True):
                    tile = jnp.where(upper, tile, NEG_INF)
                o_ref[kind, qi * GRID_W:(qi + 1) * GRID_W, kp * LANES:(kp + 1) * LANES] = tile


def _bias_tables(rpb):
    n_dx = 2 * WIN_W - 1
    assert n_dx <= BIAS_HI and BIAS_HI + n_dx <= LANES
    pad = lambda a, n: jnp.pad(a, ((0, 0), (0, 0), (0, 2), (0, n - a.shape[-1])))
    rp = jnp.concatenate([pad(rpb[:, :, :-1], BIAS_HI), pad(rpb[:, :, 1:], LANES - BIAS_HI)], axis=-1)
    rows = rp.shape[2]
    return pl.pallas_call(
        _bias_kernel,
        grid=(DEPTH, NA_HEADS),
        in_specs=[pl.BlockSpec((None, None, rows, LANES), lambda l, h: (l, h, 0, 0))],
        out_specs=pl.BlockSpec((None, N_TAB_KINDS, None, Q_BLOCK, Q_BLOCK),
                               lambda l, h: (l, 0, h, 0, 0)),
        out_shape=jax.ShapeDtypeStruct((DEPTH, N_TAB_KINDS, NA_HEADS, Q_BLOCK, Q_BLOCK), F32),
        scratch_shapes=[pltpu.VMEM((2 * WIN_H - 2, GRID_W, LANES), F32)],
        compiler_params=_params(2),
        name="bias_tables",
    )(rp)


def _layer_weights(i, w_in, sg_ln_g, sg_ln_b, w_s, b_s, w_pa, w_pb, w_o, ln1_g, ln1_b,
                   w_up, conv_w, conv_b, w_down, ln2_g, ln2_b):
    row = lambda a: a.reshape(1, -1)
    bs = jnp.repeat(b_s[i].reshape(SG_GROUPS // 2, 2, CHUNK).transpose(0, 2, 1),
                    SG_WIDTH // SG_GROUPS, axis=-1)
    ws = w_s[i].astype(BF16).reshape(SG_GROUPS // 2, 2 * CHUNK, CHUNK)
    mix = (ws, bs, w_pa[i].astype(BF16), w_pb[i].astype(BF16),
           w_o[i].astype(BF16), row(ln1_g[i]), row(ln1_b[i]))
    ffn = (w_up[i].astype(BF16), conv_w[i], row(conv_b[i]), w_down[i].astype(BF16),
           row(ln2_g[i]), row(ln2_b[i]))
    return w_in[i].astype(BF16), row(sg_ln_g[i]), row(sg_ln_b[i]), mix, ffn


def kernel(x, c, ctx, c_ctx, w_ada, b_ada, w_in, rpb, sg_ln_g, sg_ln_b, w_s, b_s, w_pa, w_pb, w_o,
           ln1_g, ln1_b, w_up, conv_w, conv_b, w_down, ln2_g, ln2_b):
    assert x.shape == (BATCH, SEQ, D_MODEL) and ctx.shape == (BATCH, CTX_LEN, D_MODEL)
    cc = jnp.concatenate([c, c_ctx[None, :], jnp.zeros((MOD_ROWS - BATCH - 1, D_MODEL), F32)], axis=0)
    mods = _modulation(cc, w_ada, b_ada)
    rope = _rope_tables()
    tabs = _bias_tables(rpb)
    xl = x.reshape(BATCH * SEQ, D_MODEL)
    xc = ctx.reshape(BATCH * CTX_LEN, D_MODEL)
    for i in range(DEPTH):
        w_in_i, lng, lnb, mix_w, ffn_w = _layer_weights(
            i, w_in, sg_ln_g, sg_ln_b, w_s, b_s, w_pa, w_pb, w_o, ln1_g, ln1_b,
            w_up, conv_w, conv_b, w_down, ln2_g, ln2_b)
        mod = mods[i]
        tab = tabs[i]
        if i < DEPTH - 1:
            c_acts = _in_projection(xc, mod, w_in_i, lng, lnb, None, mode="ctx", tm=2 * CTX_LEN,
                                    seq=2 * CTX_LEN)
            ctx_kv = (c_acts[1], c_acts[2])
        else:
            ctx_kv = _in_projection(xc, mod, w_in_i[:, NA_WIDTH:3 * NA_WIDTH], None, None, None,
                                    mode="ctx_kv", tm=2 * CTX_LEN, seq=2 * CTX_LEN)
        acts = _in_projection(xl, mod, w_in_i, lng, lnb, rope, mode="latent", tm=1024, seq=SEQ)
        xl = _mix(xl, mod, acts, ctx_kv, tab, mix_w, latent=True, n_sub=2)
        xl = _conv_ffn(xl, mod, ffn_w, latent=True, tm=512, seq=SEQ)
        if i < DEPTH - 1:
            xc = _mix(xc, mod, c_acts, None, None, mix_w, latent=False, n_sub=1)
            xc = _conv_ffn(xc, mod, ffn_w, latent=False, tm=CTX_LEN, seq=CTX_LEN)
    return xl.reshape(BATCH, SEQ, D_MODEL)
```

```python
import functools

import jax
import jax.numpy as jnp
from jax import lax
from jax.experimental import pallas as pl
from jax.experimental.pallas import tpu as pltpu

D_MODEL = 1024
BATCH = 4
SEQ = 4096
DEPTH = 2
GRID_W = 64
CTX_LEN = 256
NA_HEADS = 8
HEAD_DIM = 64
NA_WIDTH = NA_HEADS * HEAD_DIM
WIN_H = 8
WIN_W = 16
ROPE_BASE = 10000.0
SG_GROUPS = 8
SG_WIDTH = 512
CHUNK = 128
D_FF = 2816
ALPHA = (2 * DEPTH) ** 0.25
LN_EPS = 1e-5
NEG_INF = -1e30

F32 = jnp.float32
BF16 = jnp.bfloat16

LANES = 128
HEAD_PAIRS = NA_HEADS // 2
MOD_ROWS = 8
CTX_MOD_ROW = BATCH
Q_ROWS = 4
Q_BLOCK = Q_ROWS * GRID_W
K_ROWS = 3 * Q_ROWS
TAB_LOW, TAB_LOW_FULL, TAB_MID, TAB_HIGH, TAB_HIGH_FULL, TAB_MASKED = range(6)
TAB_KEY_BLOCK = (0, 0, 1, 2, 2, 0)
N_TAB_KINDS = len(TAB_KEY_BLOCK)
FF_CHUNK = 256
N_FF_CHUNKS = D_FF // FF_CHUNK
HALO = 8
VMEM_LIMIT = 56 * 1024 * 1024


def _dot(a, b):
    return jnp.dot(a, b, preferred_element_type=F32)


def _dot_nt(a, b):
    return lax.dot_general(a, b, (((1,), (1,)), ((), ())), preferred_element_type=F32)


def _layer_norm(v, g, b):
    mu = jnp.mean(v, axis=-1, keepdims=True)
    d = v - mu
    var = jnp.mean(d * d, axis=-1, keepdims=True)
    return d * lax.rsqrt(var + LN_EPS) * g + b


def _params(n_axes):
    return pltpu.CompilerParams(dimension_semantics=("arbitrary",) * n_axes,
                                vmem_limit_bytes=VMEM_LIMIT)


def _resident(shape):
    zeros = (0,) * len(shape)
    return pl.BlockSpec(shape, lambda *_: zeros, pipeline_mode=pl.Buffered(1))


def _mod_kernel(cc_ref, w_ref, b_ref, o_ref):
    s = jax.nn.silu(cc_ref[...]).astype(BF16)
    o_ref[...] = _dot(s, w_ref[...].astype(BF16)) + b_ref[...]


def _modulation(cc, w_ada, b_ada):
    tn = 1536
    n6 = 6 * D_MODEL
    out = pl.pallas_call(
        _mod_kernel,
        grid=(DEPTH, n6 // tn),
        in_specs=[pl.BlockSpec((MOD_ROWS, D_MODEL), lambda l, n: (0, 0)),
                  pl.BlockSpec((None, D_MODEL, tn), lambda l, n: (l, 0, n)),
                  pl.BlockSpec((None, 1, tn), lambda l, n: (l, 0, n))],
        out_specs=pl.BlockSpec((None, MOD_ROWS, tn), lambda l, n: (l, 0, n)),
        out_shape=jax.ShapeDtypeStruct((DEPTH, MOD_ROWS, n6), F32),
        compiler_params=_params(2),
        name="adaln_modulation",
    )(cc, w_ada, b_ada.reshape(DEPTH, 1, n6))
    return out.reshape(DEPTH, MOD_ROWS, 6, D_MODEL)


def _rope_store(z, cos, sin, first_half, out_ref, scale):
    for j in range(NA_WIDTH // LANES):
        zj = z[:, j * LANES:(j + 1) * LANES]
        partner = jnp.where(first_half, pltpu.roll(zj, LANES - 16, 1), pltpu.roll(zj, 16, 1))
        r = zj * cos + partner * sin
        if scale != 1.0:
            r = r * scale
        out_ref[:, j * LANES:(j + 1) * LANES] = r.astype(out_ref.dtype)


def _inproj_kernel(*refs, mode):
    if mode == "latent":
        (x_ref, mod_ref, w_ref, lng_ref, lnb_ref, cos_ref, sin_ref,
         qr_ref, qp_ref, kr_ref, v_ref, gu_ref, svn_ref, sga_ref, sgb_ref) = refs
    elif mode == "ctx":
        (x_ref, mod_ref, w_ref, lng_ref, lnb_ref,
         qp_ref, kr_ref, v_ref, gu_ref, svn_ref, sga_ref, sgb_ref) = refs
    else:
        x_ref, mod_ref, w_ref, kr_ref, v_ref = refs

    scale = HEAD_DIM ** -0.5
    h = (x_ref[...] * (1.0 + mod_ref[1:2, :]) + mod_ref[0:1, :]).astype(BF16)

    def proj(lo, width):
        return _dot(h, w_ref[:, lo:lo + width])

    if mode == "ctx_kv":
        kr_ref[...] = proj(0, NA_WIDTH).astype(BF16)
        v_ref[...] = proj(NA_WIDTH, NA_WIDTH).astype(BF16)
        return

    zq = proj(0, NA_WIDTH)
    zk = proj(NA_WIDTH, NA_WIDTH)
    qp_ref[...] = (zq * scale).astype(BF16)
    if mode == "latent":
        cos = cos_ref[...]
        sin = sin_ref[...]
        lane = lax.broadcasted_iota(jnp.int32, cos.shape, 1)
        first_half = (lane % 32) < 16
        _rope_store(zq, cos, sin, first_half, qr_ref, scale)
        _rope_store(zk, cos, sin, first_half, kr_ref, 1.0)
    else:
        kr_ref[...] = zk.astype(BF16)
    v_ref[...] = proj(2 * NA_WIDTH, NA_WIDTH).astype(BF16)
    gu_ref[...] = jax.nn.gelu(proj(3 * NA_WIDTH, SG_WIDTH)).astype(BF16)
    sv = jax.nn.gelu(proj(3 * NA_WIDTH + SG_WIDTH, SG_WIDTH))
    svn_ref[...] = _layer_norm(sv, lng_ref[...], lnb_ref[...]).astype(BF16)
    lo = 3 * NA_WIDTH + 2 * SG_WIDTH
    sga_ref[...] = jax.nn.sigmoid(proj(lo, D_MODEL)).astype(BF16)
    sgb_ref[...] = jax.nn.sigmoid(proj(lo + D_MODEL, D_MODEL)).astype(BF16)


def _in_projection(x2d, mod, w, lng, lnb, rope, *, mode, tm, seq):
    t = x2d.shape[0]
    tiles_per_seq = seq // tm
    if mode == "latent":
        mod_map = lambda i: (i // tiles_per_seq, 0, 0)
    else:
        mod_map = lambda i: (CTX_MOD_ROW, 0, 0)
    tok = lambda width: pl.BlockSpec((tm, width), lambda i: (i, 0))
    in_specs = [tok(D_MODEL), pl.BlockSpec((None, 6, D_MODEL), mod_map), _resident(w.shape)]
    args = [x2d, mod, w]
    if mode != "ctx_kv":
        in_specs += [_resident((1, SG_WIDTH)), _resident((1, SG_WIDTH))]
        args += [lng, lnb]
    if mode == "latent":
        pos = pl.BlockSpec((tm, LANES), lambda i: (i % tiles_per_seq, 0))
        in_specs += [pos, pos]
        args += list(rope)
    widths = {"latent": [NA_WIDTH] * 6 + [D_MODEL] * 2,
              "ctx": [NA_WIDTH] * 5 + [D_MODEL] * 2,
              "ctx_kv": [NA_WIDTH] * 2}[mode]
    return pl.pallas_call(
        functools.partial(_inproj_kernel, mode=mode),
        grid=(t // tm,),
        in_specs=in_specs,
        out_specs=[tok(wd) for wd in widths],
        out_shape=[jax.ShapeDtypeStruct((t, wd), BF16) for wd in widths],
        compiler_params=_params(1),
        name="in_projection_" + mode,
    )(*args)


def _mix_kernel(*refs, latent, n_sub):
    n_nbr = n_sub + 2
    if latent:
        x_ref, mod_ref, qr_ref, qp_ref = refs[:4]
        k_refs = refs[4:4 + n_nbr]
        v_refs = refs[4 + n_nbr:4 + 2 * n_nbr]
        (kc_ref, vc_ref, tab_ref, gu_ref, svn_ref, sga_ref, sgb_ref, ws_ref, bs_ref,
         wpa_ref, wpb_ref, wo_ref, lng_ref, lnb_ref, o_ref, oa_ref, ob_ref) = refs[4 + 2 * n_nbr:]
    else:
        (x_ref, mod_ref, qp_ref, kc_ref, vc_ref, gu_ref, svn_ref, sga_ref, sgb_ref,
         ws_ref, bs_ref, wpa_ref, wpb_ref, wo_ref, lng_ref, lnb_ref,
         o_ref, oa_ref, ob_ref) = refs

    lane = lax.broadcasted_iota(jnp.int32, (Q_BLOCK, LANES), 1)
    low_half = lane < HEAD_DIM
    first_step = pl.program_id(1) == 0
    last_step = pl.program_id(1) == pl.num_programs(1) - 1

    for sub in range(n_sub):
        rows = slice(sub * Q_BLOCK, (sub + 1) * Q_BLOCK)
        tabs = [TAB_LOW, TAB_MID, TAB_HIGH]
        if latent and sub == 0:
            tabs[0] = jnp.where(first_step, TAB_MASKED, tabs[0])
            tabs[2] = jnp.where(first_step, TAB_HIGH_FULL, tabs[2])
        if latent and sub == n_sub - 1:
            tabs[0] = jnp.where(last_step, TAB_LOW_FULL, tabs[0])
            tabs[2] = jnp.where(last_step, TAB_MASKED, tabs[2])
        for p in range(HEAD_PAIRS):
            cols = slice(p * LANES, (p + 1) * LANES)

            def per_head(q):
                return jnp.concatenate([jnp.where(low_half, q, 0), jnp.where(low_half, 0, q)], axis=0)

            s_parts = [_dot_nt(per_head(qp_ref[rows, cols]), kc_ref[:, cols])]
            if latent:
                q_lat = per_head(qr_ref[rows, cols])
                for m in range(3):
                    s = _dot_nt(q_lat, k_refs[sub + m][:, cols])
                    tab = tab_ref[tabs[m], 2 * p:2 * p + 2].reshape(2 * Q_BLOCK, Q_BLOCK)
                    s_parts.append(s + tab)
            mx = functools.reduce(jnp.maximum, s_parts).max(axis=-1, keepdims=True)
            p_parts = [jnp.exp(s - mx) for s in s_parts]
            denom = functools.reduce(jnp.add, p_parts).sum(axis=-1, keepdims=True)
            acc = _dot(p_parts[0].astype(BF16), vc_ref[:, cols])
            for m in range(len(p_parts) - 1):
                acc = acc + _dot(p_parts[m + 1].astype(BF16), v_refs[sub + m][:, cols])
            o = acc / denom
            oa_ref[rows, cols] = jnp.where(low_half, o[:Q_BLOCK], o[Q_BLOCK:]).astype(BF16)

    half = lax.broadcasted_iota(jnp.int32, (CHUNK, LANES), 1) < (SG_WIDTH // SG_GROUPS)
    for c2 in range(n_sub * Q_BLOCK // (2 * CHUNK)):
        chunk_rows = [slice((2 * c2 + i) * CHUNK, (2 * c2 + i + 1) * CHUNK) for i in range(2)]
        for p in range(SG_GROUPS // 2):
            cols = slice(p * LANES, (p + 1) * LANES)
            vl = jnp.concatenate([svn_ref[r, cols] for r in chunk_rows], axis=1)
            both = _dot(ws_ref[p], vl)
            for i, r in enumerate(chunk_rows):
                lanes = slice(i * LANES, (i + 1) * LANES)
                mixed = jnp.where(half, both[:CHUNK, lanes], both[CHUNK:, lanes]) + bs_ref[p]
                ob_ref[r, cols] = (gu_ref[r, cols].astype(F32) * mixed).astype(BF16)

    for sub in range(n_sub):
        rows = slice(sub * Q_BLOCK, (sub + 1) * Q_BLOCK)
        ya = _dot(oa_ref[rows, :], wpa_ref[...])
        yb = _dot(ob_ref[rows, :], wpb_ref[...])
        y = sga_ref[rows, :].astype(F32) * ya + sgb_ref[rows, :].astype(F32) * yb
        out = _dot(y.astype(BF16), wo_ref[...])
        r = ALPHA * x_ref[rows, :] + mod_ref[2:3, :] * out
        o_ref[rows, :] = _layer_norm(r, lng_ref[...], lnb_ref[...])


def _mix(x2d, mod, acts, ctx_kv, tab, wts, *, latent, n_sub):
    t = x2d.shape[0]
    seq = SEQ if latent else CTX_LEN
    tq = n_sub * Q_BLOCK
    steps = seq // tq
    blocks_per_seq = seq // Q_BLOCK
    tok = lambda width: pl.BlockSpec((tq, width), lambda b, j: (b * steps + j, 0))
    ctx_spec = pl.BlockSpec((CTX_LEN, NA_WIDTH), lambda b, j: (b, 0))
    w_specs = [_resident(a.shape) for a in wts]
    if latent:
        qr, qp, kr, v, gu, svn, sga, sgb = acts
        kc, vc = ctx_kv
        mod_map = lambda b, j: (b, 0, 0)

        def nbr(m):
            return pl.BlockSpec(
                (Q_BLOCK, NA_WIDTH),
                lambda b, j: (b * blocks_per_seq
                              + jnp.clip(n_sub * j - 1 + m, 0, blocks_per_seq - 1), 0))

        nbrs = [nbr(m) for m in range(n_sub + 2)]
        in_specs = ([tok(D_MODEL), pl.BlockSpec((None, 6, D_MODEL), mod_map),
                     tok(NA_WIDTH), tok(NA_WIDTH)] + nbrs + nbrs
                    + [ctx_spec, ctx_spec, _resident(tab.shape)]
                    + [tok(NA_WIDTH), tok(NA_WIDTH), tok(D_MODEL), tok(D_MODEL)] + w_specs)
        args = ([x2d, mod, qr, qp] + [kr] * (n_sub + 2) + [v] * (n_sub + 2)
                + [kc, vc, tab, gu, svn, sga, sgb] + list(wts))
    else:
        qp, kc, vc, gu, svn, sga, sgb = acts
        mod_map = lambda b, j: (CTX_MOD_ROW, 0, 0)
        in_specs = ([tok(D_MODEL), pl.BlockSpec((None, 6, D_MODEL), mod_map),
                     tok(NA_WIDTH), ctx_spec, ctx_spec]
                    + [tok(NA_WIDTH), tok(NA_WIDTH), tok(D_MODEL), tok(D_MODEL)] + w_specs)
        args = [x2d, mod, qp, kc, vc, gu, svn, sga, sgb] + list(wts)
    return pl.pallas_call(
        functools.partial(_mix_kernel, latent=latent, n_sub=n_sub),
        grid=(BATCH, steps),
        in_specs=in_specs,
        out_specs=tok(D_MODEL),
        out_shape=jax.ShapeDtypeStruct((t, D_MODEL), F32),
        scratch_shapes=[pltpu.VMEM((tq, NA_WIDTH), BF16),
                        pltpu.VMEM((tq, SG_WIDTH), BF16)],
        compiler_params=_params(2),
        name="mix_latent" if latent else "mix_ctx",
    )(*args)


def _ffn_kernel(x_ref, prev_ref, next_ref, mod_ref, wu_ref, cw_ref, cb_ref, wd_ref,
                lng_ref, lnb_ref, o_ref, h_ref, act_ref, f_ref, *z_refs, tm, tiles_per_seq):
    i = pl.program_id(0)
    has_prev = (i % tiles_per_seq != 0).astype(F32)
    has_next = (i % tiles_per_seq != tiles_per_seq - 1).astype(F32)
    shift = mod_ref[3:4, :]
    scale1 = 1.0 + mod_ref[4:5, :]
    rows = tm + 2 * HALO
    h_ref[0:tm, :] = (x_ref[...] * scale1 + shift).astype(BF16)
    h_ref[tm:rows, :] = jnp.concatenate(
        [(prev_ref[...] * scale1 + shift) * has_prev,
         (next_ref[...] * scale1 + shift) * has_next], axis=0).astype(BF16)

    def conv(lo, z_ref):
        z = _dot(h_ref[...], wu_ref[:, lo:lo + FF_CHUNK])
        outs = []
        for t in range(FF_CHUNK // LANES):
            cols = slice(lo + t * LANES, lo + (t + 1) * LANES)
            zt = z[:, t * LANES:(t + 1) * LANES]
            z_ref[t, 0:HALO, :] = zt[tm:tm + HALO]
            z_ref[t, HALO:HALO + tm, :] = zt[0:tm]
            z_ref[t, HALO + tm:rows, :] = zt[tm + HALO:rows]
            outs.append(cw_ref[0:1, cols] * z_ref[t, HALO - 1:HALO - 1 + tm, :]
                        + cw_ref[1:2, cols] * zt[0:tm]
                        + cw_ref[2:3, cols] * z_ref[t, HALO + 1:HALO + 1 + tm, :]
                        + cb_ref[0:1, cols])
        return jnp.concatenate(outs, axis=1)

    def down(lo, width):
        return _dot(act_ref[:, lo:lo + width], wd_ref[lo:lo + width, :])

    assert N_FF_CHUNKS % 2 == 1
    for c in range(N_FF_CHUNKS):
        lo = c * FF_CHUNK
        zg_ref, za_ref = z_refs[2 * (c % 2)], z_refs[2 * (c % 2) + 1]
        act_ref[:, lo:lo + FF_CHUNK] = (jax.nn.silu(conv(D_FF + lo, zg_ref))
                                        * conv(lo, za_ref)).astype(BF16)
        if c == 1:
            f_ref[...] = down(0, 2 * FF_CHUNK)
        elif c % 2 == 1:
            f_ref[...] += down(lo - FF_CHUNK, 2 * FF_CHUNK)
    f = f_ref[...] + down(D_FF - FF_CHUNK, FF_CHUNK)
    r = ALPHA * x_ref[...] + mod_ref[5:6, :] * f
    o_ref[...] = _layer_norm(r, lng_ref[...], lnb_ref[...])


def _conv_ffn(x2d, mod, wts, *, latent, tm, seq):
    t = x2d.shape[0]
    tiles_per_seq = seq // tm
    n_halo_blocks = t // HALO
    per_tile = tm // HALO
    if latent:
        mod_map = lambda i: (i // tiles_per_seq, 0, 0)
    else:
        mod_map = lambda i: (CTX_MOD_ROW, 0, 0)
    tok = pl.BlockSpec((tm, D_MODEL), lambda i: (i, 0))
    prev = pl.BlockSpec((HALO, D_MODEL), lambda i: (jnp.maximum(i * per_tile - 1, 0), 0))
    nxt = pl.BlockSpec((HALO, D_MODEL),
                       lambda i: (jnp.minimum((i + 1) * per_tile, n_halo_blocks - 1), 0))
    return pl.pallas_call(
        functools.partial(_ffn_kernel, tm=tm, tiles_per_seq=tiles_per_seq),
        grid=(t // tm,),
        in_specs=[tok, prev, nxt, pl.BlockSpec((None, 6, D_MODEL), mod_map)]
                 + [_resident(a.shape) for a in wts],
        out_specs=tok,
        out_shape=jax.ShapeDtypeStruct((t, D_MODEL), F32),
        scratch_shapes=[pltpu.VMEM((tm + 2 * HALO, D_MODEL), BF16),
                        pltpu.VMEM((tm, D_FF), BF16),
                        pltpu.VMEM((tm, D_MODEL), F32)]
                       + [pltpu.VMEM((FF_CHUNK // LANES, tm + 2 * HALO, LANES), F32)] * 4,
        compiler_params=_params(1),
        name="conv_ffn_latent" if latent else "conv_ffn_ctx",
    )(x2d, x2d, x2d, mod, *wts)


def _rope_tables():
    half = HEAD_DIM // 2
    nf = half // 2
    inv = ROPE_BASE ** (-jnp.arange(nf, dtype=F32) / nf)
    t = jnp.arange(SEQ)

    def part(pos):
        ang = pos.astype(F32)[:, None] * inv[None, :]
        cos = jnp.cos(ang)
        sin = jnp.sin(ang)
        return jnp.concatenate([cos, cos], axis=-1), jnp.concatenate([-sin, sin], axis=-1)

    cr, sr = part(t // GRID_W)
    cc, sc = part(t % GRID_W)
    cos = jnp.concatenate([cr, cc], axis=-1)
    sin = jnp.concatenate([sr, sc], axis=-1)
    reps = LANES // HEAD_DIM
    return jnp.tile(cos, (1, reps)), jnp.tile(sin, (1, reps))


def _row_ok(kind, qi, ki):
    if kind == TAB_MASKED:
        return False
    if kind in (TAB_LOW_FULL, TAB_MID, TAB_HIGH_FULL):
        return True
    return qi <= ki < qi + WIN_H


BIAS_HI = 32


def _bias_kernel(rp_ref, o_ref, pt_ref):
    n_pairs = 2 * WIN_H - 2
    qc = lax.broadcasted_iota(jnp.int32, (GRID_W, LANES), 0)
    lane = lax.broadcasted_iota(jnp.int32, (GRID_W, LANES), 1)
    upper = lane >= GRID_W
    kc = jnp.where(upper, lane - GRID_W, lane)
    win_start = jnp.clip(qc - WIN_W // 2, 0, GRID_W - WIN_W)
    col_ok = (kc >= win_start) & (kc < win_start + WIN_W)
    idx = jnp.clip(kc - qc + WIN_W - 1, 0, 2 * WIN_W - 2) + jnp.where(upper, BIAS_HI, 0)
    masked = jnp.full((GRID_W, LANES), NEG_INF, F32)
    for dy in range(n_pairs):
        src = jnp.broadcast_to(rp_ref[dy:dy + 1, :], (GRID_W, LANES))
        pt_ref[dy] = jnp.where(col_ok, jnp.take_along_axis(src, idx, axis=1), NEG_INF)
    for kind in range(N_TAB_KINDS):
        for qi in range(Q_ROWS):
            for kp in range(Q_ROWS // 2):
                ki = Q_ROWS * TAB_KEY_BLOCK[kind] + 2 * kp
                ok = (_row_ok(kind, qi, ki), _row_ok(kind, qi, ki + 1))
                tile = pt_ref[ki - qi + WIN_H // 2 - 1] if any(ok) else masked
                if ok == (True, False):
                    tile = jnp.where(upper, NEG_INF, tile)
                elif ok == (False, True):
                    tile = jnp.where(upper, tile, NEG_INF)
                o_ref[kind, qi * GRID_W:(qi + 1) * GRID_W, kp * LANES:(kp + 1) * LANES] = tile


def _bias_tables(rpb):
    n_dx = 2 * WIN_W - 1
    assert n_dx <= BIAS_HI and BIAS_HI + n_dx <= LANES
    pad = lambda a, n: jnp.pad(a, ((0, 0), (0, 0), (0, 2), (0, n - a.shape[-1])))
    rp = jnp.concatenate([pad(rpb[:, :, :-1], BIAS_HI), pad(rpb[:, :, 1:], LANES - BIAS_HI)], axis=-1)
    rows = rp.shape[2]
    return pl.pallas_call(
        _bias_kernel,
        grid=(DEPTH, NA_HEADS),
        in_specs=[pl.BlockSpec((None, None, rows, LANES), lambda l, h: (l, h, 0, 0))],
        out_specs=pl.BlockSpec((None, N_TAB_KINDS, None, Q_BLOCK, Q_BLOCK),
                               lambda l, h: (l, 0, h, 0, 0)),
        out_shape=jax.ShapeDtypeStruct((DEPTH, N_TAB_KINDS, NA_HEADS, Q_BLOCK, Q_BLOCK), F32),
        scratch_shapes=[pltpu.VMEM((2 * WIN_H - 2, GRID_W, LANES), F32)],
        compiler_params=_params(2),
        name="bias_tables",
    )(rp)


def _layer_weights(i, w_in, sg_ln_g, sg_ln_b, w_s, b_s, w_pa, w_pb, w_o, ln1_g, ln1_b,
                   w_up, conv_w, conv_b, w_down, ln2_g, ln2_b):
    row = lambda a: a.reshape(1, -1)
    bs = jnp.repeat(b_s[i].reshape(SG_GROUPS // 2, 2, CHUNK).transpose(0, 2, 1),
                    SG_WIDTH // SG_GROUPS, axis=-1)
    ws = w_s[i].reshape(SG_GROUPS // 2, 2 * CHUNK, CHUNK)
    mix = (ws, bs, w_pa[i], w_pb[i], w_o[i], row(ln1_g[i]), row(ln1_b[i]))
    ffn = (w_up[i], conv_w[i], row(conv_b[i]), w_down[i], row(ln2_g[i]), row(ln2_b[i]))
    return w_in[i], row(sg_ln_g[i]), row(sg_ln_b[i]), mix, ffn


def kernel(x, c, ctx, c_ctx, w_ada, b_ada, w_in, rpb, sg_ln_g, sg_ln_b, w_s, b_s, w_pa, w_pb, w_o,
           ln1_g, ln1_b, w_up, conv_w, conv_b, w_down, ln2_g, ln2_b):
    assert x.shape == (BATCH, SEQ, D_MODEL) and ctx.shape == (BATCH, CTX_LEN, D_MODEL)
    cc = jnp.concatenate([c, c_ctx[None, :], jnp.zeros((MOD_ROWS - BATCH - 1, D_MODEL), F32)], axis=0)
    mods = _modulation(cc, w_ada, b_ada)
    rope = _rope_tables()
    tabs = _bias_tables(rpb)
    w_in, w_s, w_pa, w_pb, w_o, w_up, w_down = (
        w.astype(BF16) for w in (w_in, w_s, w_pa, w_pb, w_o, w_up, w_down))
    xl = x.reshape(BATCH * SEQ, D_MODEL)
    xc = ctx.reshape(BATCH * CTX_LEN, D_MODEL)
    for i in range(DEPTH):
        w_in_i, lng, lnb, mix_w, ffn_w = _layer_weights(
            i, w_in, sg_ln_g, sg_ln_b, w_s, b_s, w_pa, w_pb, w_o, ln1_g, ln1_b,
            w_up, conv_w, conv_b, w_down, ln2_g, ln2_b)
        mod = mods[i]
        tab = tabs[i]
        if i < DEPTH - 1:
            c_acts = _in_projection(xc, mod, w_in_i, lng, lnb, None, mode="ctx", tm=2 * CTX_LEN,
                                    seq=2 * CTX_LEN)
            ctx_kv = (c_acts[1], c_acts[2])
        else:
            ctx_kv = _in_projection(xc, mod, w_in_i[:, NA_WIDTH:3 * NA_WIDTH], None, None, None,
                                    mode="ctx_kv", tm=2 * CTX_LEN, seq=2 * CTX_LEN)
        acts = _in_projection(xl, mod, w_in_i, lng, lnb, rope, mode="latent", tm=1024, seq=SEQ)
        xl = _mix(xl, mod, acts, ctx_kv, tab, mix_w, latent=True, n_sub=2)
        xl = _conv_ffn(xl, mod, ffn_w, latent=True, tm=512, seq=SEQ)
        if i < DEPTH - 1:
            xc = _mix(xc, mod, c_acts, None, None, mix_w, latent=False, n_sub=1)
            xc = _conv_ffn(xc, mod, ffn_w, latent=False, tm=CTX_LEN, seq=CTX_LEN)
    return xl.reshape(BATCH, SEQ, D_MODEL)
```

```python
import functools

import jax
import jax.numpy as jnp
from jax import lax
from jax.experimental import pallas as pl
from jax.experimental.pallas import tpu as pltpu

D_MODEL = 1024
BATCH = 4
SEQ = 4096
DEPTH = 2
GRID_W = 64
CTX_LEN = 256
NA_HEADS = 8
HEAD_DIM = 64
NA_WIDTH = NA_HEADS * HEAD_DIM
WIN_H = 8
WIN_W = 16
ROPE_BASE = 10000.0
SG_GROUPS = 8
SG_WIDTH = 512
CHUNK = 128
D_FF = 2816
ALPHA = (2 * DEPTH) ** 0.25
LN_EPS = 1e-5
NEG_INF = -1e30

F32 = jnp.float32
BF16 = jnp.bfloat16

LANES = 128
HEAD_PAIRS = NA_HEADS // 2
MOD_ROWS = 8
CTX_MOD_ROW = BATCH
Q_ROWS = 4
Q_BLOCK = Q_ROWS * GRID_W
K_ROWS = 3 * Q_ROWS
TAB_LOW, TAB_LOW_FULL, TAB_MID, TAB_HIGH, TAB_HIGH_FULL, TAB_MASKED = range(6)
TAB_KEY_BLOCK = (0, 0, 1, 2, 2, 0)
N_TAB_KINDS = len(TAB_KEY_BLOCK)
FF_CHUNK = 256
N_FF_CHUNKS = D_FF // FF_CHUNK
HALO = 8
VMEM_LIMIT = 56 * 1024 * 1024


def _dot(a, b):
    return jnp.dot(a, b, preferred_element_type=F32)


def _dot_nt(a, b):
    return lax.dot_general(a, b, (((1,), (1,)), ((), ())), preferred_element_type=F32)


def _layer_norm(v, g, b):
    mu = jnp.mean(v, axis=-1, keepdims=True)
    d = v - mu
    var = jnp.mean(d * d, axis=-1, keepdims=True)
    return d * lax.rsqrt(var + LN_EPS) * g + b


def _params(n_axes):
    return pltpu.CompilerParams(dimension_semantics=("arbitrary",) * n_axes,
                                vmem_limit_bytes=VMEM_LIMIT)


def _resident(stacked, layer, block=None, index=None):
    block = tuple(stacked.shape[1:]) if block is None else block
    index = (0,) * len(block) if index is None else index
    return pl.BlockSpec((None,) + block, lambda *_: (layer,) + index,
                        pipeline_mode=pl.Buffered(1))


def _mod_spec(layer, row_map):
    return pl.BlockSpec((None, None, 6, D_MODEL), lambda *g: (layer, row_map(*g), 0, 0))


def _mod_kernel(cc_ref, w_ref, b_ref, o_ref):
    s = jax.nn.silu(cc_ref[...]).astype(BF16)
    o_ref[...] = _dot(s, w_ref[...].astype(BF16)) + b_ref[...]


def _modulation(cc, w_ada, b_ada):
    tn = 1536
    n6 = 6 * D_MODEL
    out = pl.pallas_call(
        _mod_kernel,
        grid=(DEPTH, n6 // tn),
        in_specs=[pl.BlockSpec((MOD_ROWS, D_MODEL), lambda l, n: (0, 0)),
                  pl.BlockSpec((None, D_MODEL, tn), lambda l, n: (l, 0, n)),
                  pl.BlockSpec((None, 1, tn), lambda l, n: (l, 0, n))],
        out_specs=pl.BlockSpec((None, MOD_ROWS, tn), lambda l, n: (l, 0, n)),
        out_shape=jax.ShapeDtypeStruct((DEPTH, MOD_ROWS, n6), F32),
        compiler_params=_params(2),
        name="adaln_modulation",
    )(cc, w_ada, b_ada.reshape(DEPTH, 1, n6))
    return out.reshape(DEPTH, MOD_ROWS, 6, D_MODEL)


def _rope_store(z, cos, sin, first_half, out_ref, scale):
    for j in range(NA_WIDTH // LANES):
        zj = z[:, j * LANES:(j + 1) * LANES]
        partner = jnp.where(first_half, pltpu.roll(zj, LANES - 16, 1), pltpu.roll(zj, 16, 1))
        r = zj * cos + partner * sin
        if scale != 1.0:
            r = r * scale
        out_ref[:, j * LANES:(j + 1) * LANES] = r.astype(out_ref.dtype)


def _inproj_kernel(*refs, mode):
    if mode == "latent":
        (x_ref, mod_ref, w_ref, lng_ref, lnb_ref, cos_ref, sin_ref,
         qr_ref, qp_ref, kr_ref, v_ref, gu_ref, svn_ref, sga_ref, sgb_ref) = refs
    elif mode == "ctx":
        (x_ref, mod_ref, w_ref, lng_ref, lnb_ref,
         qp_ref, kr_ref, v_ref, gu_ref, svn_ref, sga_ref, sgb_ref) = refs
    else:
        x_ref, mod_ref, wk_ref, wv_ref, kr_ref, v_ref = refs

    scale = HEAD_DIM ** -0.5
    h = (x_ref[...] * (1.0 + mod_ref[1:2, :]) + mod_ref[0:1, :]).astype(BF16)

    if mode == "ctx_kv":
        kr_ref[...] = _dot(h, wk_ref[...]).astype(BF16)
        v_ref[...] = _dot(h, wv_ref[...]).astype(BF16)
        return

    def proj(lo, width):
        return _dot(h, w_ref[:, lo:lo + width])

    zq = proj(0, NA_WIDTH)
    zk = proj(NA_WIDTH, NA_WIDTH)
    qp_ref[...] = (zq * scale).astype(BF16)
    if mode == "latent":
        cos = cos_ref[...]
        sin = sin_ref[...]
        lane = lax.broadcasted_iota(jnp.int32, cos.shape, 1)
        first_half = (lane % 32) < 16
        _rope_store(zq, cos, sin, first_half, qr_ref, scale)
        _rope_store(zk, cos, sin, first_half, kr_ref, 1.0)
    else:
        kr_ref[...] = zk.astype(BF16)
    v_ref[...] = proj(2 * NA_WIDTH, NA_WIDTH).astype(BF16)
    gu_ref[...] = jax.nn.gelu(proj(3 * NA_WIDTH, SG_WIDTH)).astype(BF16)
    sv = jax.nn.gelu(proj(3 * NA_WIDTH + SG_WIDTH, SG_WIDTH))
    svn_ref[...] = _layer_norm(sv, lng_ref[...], lnb_ref[...]).astype(BF16)
    lo = 3 * NA_WIDTH + 2 * SG_WIDTH
    sga_ref[...] = jax.nn.sigmoid(proj(lo, D_MODEL)).astype(BF16)
    sgb_ref[...] = jax.nn.sigmoid(proj(lo + D_MODEL, D_MODEL)).astype(BF16)


def _in_projection(x2d, mods, w, lng, lnb, rope, *, layer, mode, tm, seq):
    t = x2d.shape[0]
    tiles_per_seq = seq // tm
    if mode == "latent":
        mod_spec = _mod_spec(layer, lambda i: i // tiles_per_seq)
    else:
        mod_spec = _mod_spec(layer, lambda i: CTX_MOD_ROW)
    tok = lambda width: pl.BlockSpec((tm, width), lambda i: (i, 0))
    if mode == "ctx_kv":
        in_specs = [tok(D_MODEL), mod_spec,
                    _resident(w, layer, (D_MODEL, NA_WIDTH), (0, 1)),
                    _resident(w, layer, (D_MODEL, NA_WIDTH), (0, 2))]
        args = [x2d, mods, w, w]
    else:
        in_specs = [tok(D_MODEL), mod_spec, _resident(w, layer),
                    _resident(lng, layer), _resident(lnb, layer)]
        args = [x2d, mods, w, lng, lnb]
    if mode == "latent":
        pos = pl.BlockSpec((tm, LANES), lambda i: (i % tiles_per_seq, 0))
        in_specs += [pos, pos]
        args += list(rope)
    widths = {"latent": [NA_WIDTH] * 6 + [D_MODEL] * 2,
              "ctx": [NA_WIDTH] * 5 + [D_MODEL] * 2,
              "ctx_kv": [NA_WIDTH] * 2}[mode]
    return pl.pallas_call(
        functools.partial(_inproj_kernel, mode=mode),
        grid=(t // tm,),
        in_specs=in_specs,
        out_specs=[tok(wd) for wd in widths],
        out_shape=[jax.ShapeDtypeStruct((t, wd), BF16) for wd in widths],
        compiler_params=_params(1),
        name="in_projection_" + mode,
    )(*args)


def _mix_kernel(*refs, latent, n_sub):
    n_nbr = n_sub + 2
    if latent:
        x_ref, mod_ref, qr_ref, qp_ref = refs[:4]
        k_refs = refs[4:4 + n_nbr]
        v_refs = refs[4 + n_nbr:4 + 2 * n_nbr]
        (kc_ref, vc_ref, tab_ref, gu_ref, svn_ref, sga_ref, sgb_ref, ws_ref, bs_ref,
         wpa_ref, wpb_ref, wo_ref, lng_ref, lnb_ref, o_ref, oa_ref, ob_ref) = refs[4 + 2 * n_nbr:]
    else:
        (x_ref, mod_ref, qp_ref, kc_ref, vc_ref, gu_ref, svn_ref, sga_ref, sgb_ref,
         ws_ref, bs_ref, wpa_ref, wpb_ref, wo_ref, lng_ref, lnb_ref,
         o_ref, oa_ref, ob_ref) = refs

    lane = lax.broadcasted_iota(jnp.int32, (Q_BLOCK, LANES), 1)
    low_half = lane < HEAD_DIM
    first_step = pl.program_id(1) == 0
    last_step = pl.program_id(1) == pl.num_programs(1) - 1

    for sub in range(n_sub):
        rows = slice(sub * Q_BLOCK, (sub + 1) * Q_BLOCK)
        tabs = [TAB_LOW, TAB_MID, TAB_HIGH]
        if latent and sub == 0:
            tabs[0] = jnp.where(first_step, TAB_MASKED, tabs[0])
            tabs[2] = jnp.where(first_step, TAB_HIGH_FULL, tabs[2])
        if latent and sub == n_sub - 1:
            tabs[0] = jnp.where(last_step, TAB_LOW_FULL, tabs[0])
            tabs[2] = jnp.where(last_step, TAB_MASKED, tabs[2])
        for p in range(HEAD_PAIRS):
            cols = slice(p * LANES, (p + 1) * LANES)

            def per_head(q):
                return jnp.concatenate([jnp.where(low_half, q, 0), jnp.where(low_half, 0, q)], axis=0)

            s_parts = [_dot_nt(per_head(qp_ref[rows, cols]), kc_ref[:, cols])]
            if latent:
                q_lat = per_head(qr_ref[rows, cols])
                for m in range(3):
                    s = _dot_nt(q_lat, k_refs[sub + m][:, cols])
                    tab = tab_ref[tabs[m], 2 * p:2 * p + 2].reshape(2 * Q_BLOCK, Q_BLOCK)
                    s_parts.append(s + tab)
            mx = functools.reduce(jnp.maximum, s_parts).max(axis=-1, keepdims=True)
            p_parts = [jnp.exp(s - mx) for s in s_parts]
            denom = functools.reduce(jnp.add, p_parts).sum(axis=-1, keepdims=True)
            acc = _dot(p_parts[0].astype(BF16), vc_ref[:, cols])
            for m in range(len(p_parts) - 1):
                acc = acc + _dot(p_parts[m + 1].astype(BF16), v_refs[sub + m][:, cols])
            o = acc / denom
            oa_ref[rows, cols] = jnp.where(low_half, o[:Q_BLOCK], o[Q_BLOCK:]).astype(BF16)

    half = lax.broadcasted_iota(jnp.int32, (CHUNK, LANES), 1) < (SG_WIDTH // SG_GROUPS)
    for c2 in range(n_sub * Q_BLOCK // (2 * CHUNK)):
        chunk_rows = [slice((2 * c2 + i) * CHUNK, (2 * c2 + i + 1) * CHUNK) for i in range(2)]
        for p in range(SG_GROUPS // 2):
            cols = slice(p * LANES, (p + 1) * LANES)
            vl = jnp.concatenate([svn_ref[r, cols] for r in chunk_rows], axis=1)
            both = _dot(ws_ref[p], vl)
            for i, r in enumerate(chunk_rows):
                lanes = slice(i * LANES, (i + 1) * LANES)
                mixed = jnp.where(half, both[:CHUNK, lanes], both[CHUNK:, lanes]) + bs_ref[p]
                ob_ref[r, cols] = (gu_ref[r, cols].astype(F32) * mixed).astype(BF16)

    for sub in range(n_sub):
        rows = slice(sub * Q_BLOCK, (sub + 1) * Q_BLOCK)
        ya = _dot(oa_ref[rows, :], wpa_ref[...])
        yb = _dot(ob_ref[rows, :], wpb_ref[...])
        y = sga_ref[rows, :].astype(F32) * ya + sgb_ref[rows, :].astype(F32) * yb
        out = _dot(y.astype(BF16), wo_ref[...])
        r = ALPHA * x_ref[rows, :] + mod_ref[2:3, :] * out
        o_ref[rows, :] = _layer_norm(r, lng_ref[...], lnb_ref[...])


def _mix(x2d, mods, acts, ctx_kv, tabs, wts, *, layer, latent, n_sub):
    t = x2d.shape[0]
    seq = SEQ if latent else CTX_LEN
    tq = n_sub * Q_BLOCK
    steps = seq // tq
    blocks_per_seq = seq // Q_BLOCK
    tok = lambda width: pl.BlockSpec((tq, width), lambda b, j: (b * steps + j, 0))
    ctx_spec = pl.BlockSpec((CTX_LEN, NA_WIDTH), lambda b, j: (b, 0))
    w_specs = [_resident(a, layer) for a in wts]
    if latent:
        qr, qp, kr, v, gu, svn, sga, sgb = acts
        kc, vc = ctx_kv

        def nbr(m):
            return pl.BlockSpec(
                (Q_BLOCK, NA_WIDTH),
                lambda b, j: (b * blocks_per_seq
                              + jnp.clip(n_sub * j - 1 + m, 0, blocks_per_seq - 1), 0))

        nbrs = [nbr(m) for m in range(n_sub + 2)]
        in_specs = ([tok(D_MODEL), _mod_spec(layer, lambda b, j: b),
                     tok(NA_WIDTH), tok(NA_WIDTH)] + nbrs + nbrs
                    + [ctx_spec, ctx_spec, _resident(tabs, layer)]
                    + [tok(NA_WIDTH), tok(NA_WIDTH), tok(D_MODEL), tok(D_MODEL)] + w_specs)
        args = ([x2d, mods, qr, qp] + [kr] * (n_sub + 2) + [v] * (n_sub + 2)
                + [kc, vc, tabs, gu, svn, sga, sgb] + list(wts))
    else:
        qp, kc, vc, gu, svn, sga, sgb = acts
        in_specs = ([tok(D_MODEL), _mod_spec(layer, lambda b, j: CTX_MOD_ROW),
                     tok(NA_WIDTH), ctx_spec, ctx_spec]
                    + [tok(NA_WIDTH), tok(NA_WIDTH), tok(D_MODEL), tok(D_MODEL)] + w_specs)
        args = [x2d, mods, qp, kc, vc, gu, svn, sga, sgb] + list(wts)
    return pl.pallas_call(
        functools.partial(_mix_kernel, latent=latent, n_sub=n_sub),
        grid=(BATCH, steps),
        in_specs=in_specs,
        out_specs=tok(D_MODEL),
        out_shape=jax.ShapeDtypeStruct((t, D_MODEL), F32),
        scratch_shapes=[pltpu.VMEM((tq, NA_WIDTH), BF16),
                        pltpu.VMEM((tq, SG_WIDTH), BF16)],
        compiler_params=_params(2),
        name="mix_latent" if latent else "mix_ctx",
    )(*args)


def _ffn_kernel(x_ref, prev_ref, next_ref, mod_ref, wu_ref, cw_ref, cb_ref, wd_ref,
                lng_ref, lnb_ref, o_ref, h_ref, act_ref, f_ref, *z_refs, tm, tiles_per_seq):
    i = pl.program_id(0)
    has_prev = (i % tiles_per_seq != 0).astype(F32)
    has_next = (i % tiles_per_seq != tiles_per_seq - 1).astype(F32)
    shift = mod_ref[3:4, :]
    scale1 = 1.0 + mod_ref[4:5, :]
    rows = tm + 2 * HALO
    h_ref[0:tm, :] = (x_ref[...] * scale1 + shift).astype(BF16)
    h_ref[tm:rows, :] = jnp.concatenate(
        [(prev_ref[...] * scale1 + shift) * has_prev,
         (next_ref[...] * scale1 + shift) * has_next], axis=0).astype(BF16)

    def conv(lo, z_ref):
        z = _dot(h_ref[...], wu_ref[:, lo:lo + FF_CHUNK])
        outs = []
        for t in range(FF_CHUNK // LANES):
            cols = slice(lo + t * LANES, lo + (t + 1) * LANES)
            zt = z[:, t * LANES:(t + 1) * LANES]
            z_ref[t, 0:HALO, :] = zt[tm:tm + HALO]
            z_ref[t, HALO:HALO + tm, :] = zt[0:tm]
            z_ref[t, HALO + tm:rows, :] = zt[tm + HALO:rows]
            outs.append(cw_ref[0:1, cols] * z_ref[t, HALO - 1:HALO - 1 + tm, :]
                        + cw_ref[1:2, cols] * zt[0:tm]
                        + cw_ref[2:3, cols] * z_ref[t, HALO + 1:HALO + 1 + tm, :]
                        + cb_ref[0:1, cols])
        return jnp.concatenate(outs, axis=1)

    def down(lo, width):
        return _dot(act_ref[:, lo:lo + width], wd_ref[lo:lo + width, :])

    assert N_FF_CHUNKS % 2 == 1
    for c in range(N_FF_CHUNKS):
        lo = c * FF_CHUNK
        zg_ref, za_ref = z_refs[2 * (c % 2)], z_refs[2 * (c % 2) + 1]
        act_ref[:, lo:lo + FF_CHUNK] = (jax.nn.silu(conv(D_FF + lo, zg_ref))
                                        * conv(lo, za_ref)).astype(BF16)
        if c == 1:
            f_ref[...] = down(0, 2 * FF_CHUNK)
        elif c % 2 == 1:
            f_ref[...] += down(lo - FF_CHUNK, 2 * FF_CHUNK)
    f = f_ref[...] + down(D_FF - FF_CHUNK, FF_CHUNK)
    r = ALPHA * x_ref[...] + mod_ref[5:6, :] * f
    o_ref[...] = _layer_norm(r, lng_ref[...], lnb_ref[...])


def _conv_ffn(x2d, mods, wts, *, layer, latent, tm, seq):
    t = x2d.shape[0]
    tiles_per_seq = seq // tm
    n_halo_blocks = t // HALO
    per_tile = tm // HALO
    if latent:
        mod_spec = _mod_spec(layer, lambda i: i // tiles_per_seq)
    else:
        mod_spec = _mod_spec(layer, lambda i: CTX_MOD_ROW)
    tok = pl.BlockSpec((tm, D_MODEL), lambda i: (i, 0))
    prev = pl.BlockSpec((HALO, D_MODEL), lambda i: (jnp.maximum(i * per_tile - 1, 0), 0))
    nxt = pl.BlockSpec((HALO, D_MODEL),
                       lambda i: (jnp.minimum((i + 1) * per_tile, n_halo_blocks - 1), 0))
    return pl.pallas_call(
        functools.partial(_ffn_kernel, tm=tm, tiles_per_seq=tiles_per_seq),
        grid=(t // tm,),
        in_specs=[tok, prev, nxt, mod_spec] + [_resident(a, layer) for a in wts],
        out_specs=tok,
        out_shape=jax.ShapeDtypeStruct((t, D_MODEL), F32),
        scratch_shapes=[pltpu.VMEM((tm + 2 * HALO, D_MODEL), BF16),
                        pltpu.VMEM((tm, D_FF), BF16),
                        pltpu.VMEM((tm, D_MODEL), F32)]
                       + [pltpu.VMEM((FF_CHUNK // LANES, tm + 2 * HALO, LANES), F32)] * 4,
        compiler_params=_params(1),
        name="conv_ffn_latent" if latent else "conv_ffn_ctx",
    )(x2d, x2d, x2d, mods, *wts)


def _rope_tables():
    half = HEAD_DIM // 2
    nf = half // 2
    inv = ROPE_BASE ** (-jnp.arange(nf, dtype=F32) / nf)
    t = jnp.arange(SEQ)

    def part(pos):
        ang = pos.astype(F32)[:, None] * inv[None, :]
        cos = jnp.cos(ang)
        sin = jnp.sin(ang)
        return jnp.concatenate([cos, cos], axis=-1), jnp.concatenate([-sin, sin], axis=-1)

    cr, sr = part(t // GRID_W)
    cc, sc = part(t % GRID_W)
    cos = jnp.concatenate([cr, cc], axis=-1)
    sin = jnp.concatenate([sr, sc], axis=-1)
    reps = LANES // HEAD_DIM
    return jnp.tile(cos, (1, reps)), jnp.tile(sin, (1, reps))


def _row_ok(kind, qi, ki):
    if kind == TAB_MASKED:
        return False
    if kind in (TAB_LOW_FULL, TAB_MID, TAB_HIGH_FULL):
        return True
    return qi <= ki < qi + WIN_H


BIAS_HI = 32


def _bias_kernel(rp_ref, o_ref, pt_ref):
    n_pairs = 2 * WIN_H - 2
    qc = lax.broadcasted_iota(jnp.int32, (GRID_W, LANES), 0)
    lane = lax.broadcasted_iota(jnp.int32, (GRID_W, LANES), 1)
    upper = lane >= GRID_W
    kc = jnp.where(upper, lane - GRID_W, lane)
    win_start = jnp.clip(qc - WIN_W // 2, 0, GRID_W - WIN_W)
    col_ok = (kc >= win_start) & (kc < win_start + WIN_W)
    idx = jnp.clip(kc - qc + WIN_W - 1, 0, 2 * WIN_W - 2) + jnp.where(upper, BIAS_HI, 0)
    masked = jnp.full((GRID_W, LANES), NEG_INF, F32)
    for dy in range(n_pairs):
        src = jnp.broadcast_to(rp_ref[dy:dy + 1, :], (GRID_W, LANES))
        pt_ref[dy] = jnp.where(col_ok, jnp.take_along_axis(src, idx, axis=1), NEG_INF)
    for kind in range(N_TAB_KINDS):
        for qi in range(Q_ROWS):
            for kp in range(Q_ROWS // 2):
                ki = Q_ROWS * TAB_KEY_BLOCK[kind] + 2 * kp
                ok = (_row_ok(kind, qi, ki), _row_ok(kind, qi, ki + 1))
                tile = pt_ref[ki - qi + WIN_H // 2 - 1] if any(ok) else masked
                if ok == (True, False):
                    tile = jnp.where(upper, NEG_INF, tile)
                elif ok == (False, True):
                    tile = jnp.where(upper, tile, NEG_INF)
                o_ref[kind, qi * GRID_W:(qi + 1) * GRID_W, kp * LANES:(kp + 1) * LANES] = tile


def _bias_tables(rpb):
    n_dx = 2 * WIN_W - 1
    assert n_dx <= BIAS_HI and BIAS_HI + n_dx <= LANES
    pad = lambda a, n: jnp.pad(a, ((0, 0), (0, 0), (0, 2), (0, n - a.shape[-1])))
    rp = jnp.concatenate([pad(rpb[:, :, :-1], BIAS_HI), pad(rpb[:, :, 1:], LANES - BIAS_HI)], axis=-1)
    rows = rp.shape[2]
    return pl.pallas_call(
        _bias_kernel,
        grid=(DEPTH, NA_HEADS),
        in_specs=[pl.BlockSpec((None, None, rows, LANES), lambda l, h: (l, h, 0, 0))],
        out_specs=pl.BlockSpec((None, N_TAB_KINDS, None, Q_BLOCK, Q_BLOCK),
                               lambda l, h: (l, 0, h, 0, 0)),
        out_shape=jax.ShapeDtypeStruct((DEPTH, N_TAB_KINDS, NA_HEADS, Q_BLOCK, Q_BLOCK), F32),
        scratch_shapes=[pltpu.VMEM((2 * WIN_H - 2, GRID_W, LANES), F32)],
        compiler_params=_params(2),
        name="bias_tables",
    )(rp)


def _stacked_weights(w_in, sg_ln_g, sg_ln_b, w_s, b_s, w_pa, w_pb, w_o, ln1_g, ln1_b,
                     w_up, conv_w, conv_b, w_down, ln2_g, ln2_b):
    row = lambda a: a.reshape(DEPTH, 1, -1)
    bf16 = lambda a: a.astype(BF16)
    bs = jnp.repeat(b_s.reshape(DEPTH, SG_GROUPS // 2, 2, CHUNK).transpose(0, 1, 3, 2),
                    SG_WIDTH // SG_GROUPS, axis=-1)
    ws = bf16(w_s).reshape(DEPTH, SG_GROUPS // 2, 2 * CHUNK, CHUNK)
    inproj = (bf16(w_in), row(sg_ln_g), row(sg_ln_b))
    mix = (ws, bs, bf16(w_pa), bf16(w_pb), bf16(w_o), row(ln1_g), row(ln1_b))
    ffn = (bf16(w_up), conv_w, row(conv_b), bf16(w_down), row(ln2_g), row(ln2_b))
    return inproj, mix, ffn


def kernel(x, c, ctx, c_ctx, w_ada, b_ada, w_in, rpb, sg_ln_g, sg_ln_b, w_s, b_s, w_pa, w_pb, w_o,
           ln1_g, ln1_b, w_up, conv_w, conv_b, w_down, ln2_g, ln2_b):
    assert x.shape == (BATCH, SEQ, D_MODEL) and ctx.shape == (BATCH, CTX_LEN, D_MODEL)
    cc = jnp.concatenate([c, c_ctx[None, :], jnp.zeros((MOD_ROWS - BATCH - 1, D_MODEL), F32)], axis=0)
    mods = _modulation(cc, w_ada, b_ada)
    rope = _rope_tables()
    tabs = _bias_tables(rpb)
    (w_in_b, lng, lnb), mix_w, ffn_w = _stacked_weights(
        w_in, sg_ln_g, sg_ln_b, w_s, b_s, w_pa, w_pb, w_o, ln1_g, ln1_b,
        w_up, conv_w, conv_b, w_down, ln2_g, ln2_b)
    xl = x.reshape(BATCH * SEQ, D_MODEL)
    xc = ctx.reshape(BATCH * CTX_LEN, D_MODEL)
    ctx_tiles = dict(tm=2 * CTX_LEN, seq=2 * CTX_LEN)
    for i in range(DEPTH):
        if i < DEPTH - 1:
            c_acts = _in_projection(xc, mods, w_in_b, lng, lnb, None, layer=i, mode="ctx",
                                    **ctx_tiles)
            ctx_kv = (c_acts[1], c_acts[2])
        else:
            ctx_kv = _in_projection(xc, mods, w_in_b, None, None, None, layer=i, mode="ctx_kv",
                                    **ctx_tiles)
        acts = _in_projection(xl, mods, w_in_b, lng, lnb, rope, layer=i, mode="latent", tm=1024,
                              seq=SEQ)
        xl = _mix(xl, mods, acts, ctx_kv, tabs, mix_w, layer=i, latent=True, n_sub=2)
        xl = _conv_ffn(xl, mods, ffn_w, layer=i, latent=True, tm=512, seq=SEQ)
        if i < DEPTH - 1:
            xc = _mix(xc, mods, c_acts, None, None, mix_w, layer=i, latent=False, n_sub=1)
            xc = _conv_ffn(xc, mods, ffn_w, layer=i, latent=False, tm=CTX_LEN, seq=CTX_LEN)
    return xl.reshape(BATCH, SEQ, D_MODEL)
```

```python
import functools

import jax
import jax.numpy as jnp
from jax import lax
from jax.experimental import pallas as pl
from jax.experimental.pallas import tpu as pltpu

D_MODEL = 1024
BATCH = 4
SEQ = 4096
DEPTH = 2
GRID_W = 64
CTX_LEN = 256
NA_HEADS = 8
HEAD_DIM = 64
NA_WIDTH = NA_HEADS * HEAD_DIM
WIN_H = 8
WIN_W = 16
ROPE_BASE = 10000.0
SG_GROUPS = 8
SG_WIDTH = 512
CHUNK = 128
D_FF = 2816
ALPHA = (2 * DEPTH) ** 0.25
LN_EPS = 1e-5
NEG_INF = -1e30

F32 = jnp.float32
BF16 = jnp.bfloat16

LANES = 128
HEAD_PAIRS = NA_HEADS // 2
MOD_ROWS = 8
CTX_MOD_ROW = BATCH
Q_ROWS = 4
Q_BLOCK = Q_ROWS * GRID_W
K_ROWS = 3 * Q_ROWS
TAB_LOW, TAB_LOW_FULL, TAB_MID, TAB_HIGH, TAB_HIGH_FULL, TAB_MASKED = range(6)
TAB_KEY_BLOCK = (0, 0, 1, 2, 2, 0)
N_TAB_KINDS = len(TAB_KEY_BLOCK)
FF_CHUNK = 256
N_FF_CHUNKS = D_FF // FF_CHUNK
HALO = 8
VMEM_LIMIT = 56 * 1024 * 1024


def _dot(a, b):
    return jnp.dot(a, b, preferred_element_type=F32)


def _dot_nt(a, b):
    return lax.dot_general(a, b, (((1,), (1,)), ((), ())), preferred_element_type=F32)


def _layer_norm(v, g, b):
    mu = jnp.mean(v, axis=-1, keepdims=True)
    d = v - mu
    var = jnp.mean(d * d, axis=-1, keepdims=True)
    return d * lax.rsqrt(var + LN_EPS) * g + b


def _params(n_axes):
    return pltpu.CompilerParams(dimension_semantics=("arbitrary",) * n_axes,
                                vmem_limit_bytes=VMEM_LIMIT)


def _resident(stacked, layer, block=None, index=None):
    block = tuple(stacked.shape[1:]) if block is None else block
    index = (0,) * len(block) if index is None else index
    return pl.BlockSpec((None,) + block, lambda *_: (layer,) + index,
                        pipeline_mode=pl.Buffered(1))


def _mod_spec(layer, row_map):
    return pl.BlockSpec((None, None, 6, D_MODEL), lambda *g: (layer, row_map(*g), 0, 0))


def _mod_kernel(cc_ref, w_ref, b_ref, o_ref):
    s = jax.nn.silu(cc_ref[...]).astype(BF16)
    o_ref[...] = _dot(s, w_ref[...].astype(BF16)) + b_ref[...]


def _modulation(cc, w_ada, b_ada):
    tn = 1536
    n6 = 6 * D_MODEL
    out = pl.pallas_call(
        _mod_kernel,
        grid=(DEPTH, n6 // tn),
        in_specs=[pl.BlockSpec((MOD_ROWS, D_MODEL), lambda l, n: (0, 0)),
                  pl.BlockSpec((None, D_MODEL, tn), lambda l, n: (l, 0, n)),
                  pl.BlockSpec((None, 1, tn), lambda l, n: (l, 0, n))],
        out_specs=pl.BlockSpec((None, MOD_ROWS, tn), lambda l, n: (l, 0, n)),
        out_shape=jax.ShapeDtypeStruct((DEPTH, MOD_ROWS, n6), F32),
        compiler_params=_params(2),
        name="adaln_modulation",
    )(cc, w_ada, b_ada.reshape(DEPTH, 1, n6))
    return out.reshape(DEPTH, MOD_ROWS, 6, D_MODEL)


def _rope_store(z, cos, sin, first_half, out_ref, scale):
    for j in range(NA_WIDTH // LANES):
        zj = z[:, j * LANES:(j + 1) * LANES]
        partner = jnp.where(first_half, pltpu.roll(zj, LANES - 16, 1), pltpu.roll(zj, 16, 1))
        r = zj * cos + partner * sin
        if scale != 1.0:
            r = r * scale
        out_ref[:, j * LANES:(j + 1) * LANES] = r.astype(out_ref.dtype)


def _inproj_kernel(*refs, mode):
    if mode == "latent":
        (x_ref, mod_ref, w_ref, lng_ref, lnb_ref, cos_ref, sin_ref,
         qr_ref, qp_ref, kr_ref, v_ref, gu_ref, svn_ref, sga_ref, sgb_ref) = refs
    elif mode == "ctx":
        (x_ref, mod_ref, w_ref, lng_ref, lnb_ref,
         qp_ref, kr_ref, v_ref, gu_ref, svn_ref, sga_ref, sgb_ref) = refs
    else:
        x_ref, mod_ref, wk_ref, wv_ref, kr_ref, v_ref = refs

    scale = HEAD_DIM ** -0.5
    h = (x_ref[...] * (1.0 + mod_ref[1:2, :]) + mod_ref[0:1, :]).astype(BF16)

    if mode == "ctx_kv":
        kr_ref[...] = _dot(h, wk_ref[...]).astype(BF16)
        v_ref[...] = _dot(h, wv_ref[...]).astype(BF16)
        return

    def proj(lo, width):
        return _dot(h, w_ref[:, lo:lo + width])

    zq = proj(0, NA_WIDTH)
    zk = proj(NA_WIDTH, NA_WIDTH)
    qp_ref[...] = (zq * scale).astype(BF16)
    if mode == "latent":
        cos = cos_ref[...]
        sin = sin_ref[...]
        lane = lax.broadcasted_iota(jnp.int32, cos.shape, 1)
        first_half = (lane % 32) < 16
        _rope_store(zq, cos, sin, first_half, qr_ref, scale)
        _rope_store(zk, cos, sin, first_half, kr_ref, 1.0)
    else:
        kr_ref[...] = zk.astype(BF16)
    v_ref[...] = proj(2 * NA_WIDTH, NA_WIDTH).astype(BF16)
    gu_ref[...] = jax.nn.gelu(proj(3 * NA_WIDTH, SG_WIDTH)).astype(BF16)
    sv = jax.nn.gelu(proj(3 * NA_WIDTH + SG_WIDTH, SG_WIDTH))
    svn_ref[...] = _layer_norm(sv, lng_ref[...], lnb_ref[...]).astype(BF16)
    lo = 3 * NA_WIDTH + 2 * SG_WIDTH
    sga_ref[...] = jax.nn.sigmoid(proj(lo, D_MODEL)).astype(BF16)
    sgb_ref[...] = jax.nn.sigmoid(proj(lo + D_MODEL, D_MODEL)).astype(BF16)


def _in_projection(x2d, mods, w, lng, lnb, rope, *, layer, mode, tm, seq):
    t = x2d.shape[0]
    tiles_per_seq = seq // tm
    if mode == "latent":
        mod_spec = _mod_spec(layer, lambda i: i // tiles_per_seq)
    else:
        mod_spec = _mod_spec(layer, lambda i: CTX_MOD_ROW)
    tok = lambda width: pl.BlockSpec((tm, width), lambda i: (i, 0))
    if mode == "ctx_kv":
        in_specs = [tok(D_MODEL), mod_spec,
                    _resident(w, layer, (D_MODEL, NA_WIDTH), (0, 1)),
                    _resident(w, layer, (D_MODEL, NA_WIDTH), (0, 2))]
        args = [x2d, mods, w, w]
    else:
        in_specs = [tok(D_MODEL), mod_spec, _resident(w, layer),
                    _resident(lng, layer), _resident(lnb, layer)]
        args = [x2d, mods, w, lng, lnb]
    if mode == "latent":
        pos = pl.BlockSpec((tm, LANES), lambda i: (i % tiles_per_seq, 0))
        in_specs += [pos, pos]
        args += list(rope)
    widths = {"latent": [NA_WIDTH] * 6 + [D_MODEL] * 2,
              "ctx": [NA_WIDTH] * 5 + [D_MODEL] * 2,
              "ctx_kv": [NA_WIDTH] * 2}[mode]
    return pl.pallas_call(
        functools.partial(_inproj_kernel, mode=mode),
        grid=(t // tm,),
        in_specs=in_specs,
        out_specs=[tok(wd) for wd in widths],
        out_shape=[jax.ShapeDtypeStruct((t, wd), BF16) for wd in widths],
        compiler_params=_params(1),
        name="in_projection_" + mode,
    )(*args)


def _mix_kernel(*refs, latent, n_sub):
    n_nbr = n_sub + 2
    if latent:
        x_ref, mod_ref, qr_ref, qp_ref = refs[:4]
        k_refs = refs[4:4 + n_nbr]
        v_refs = refs[4 + n_nbr:4 + 2 * n_nbr]
        (kc_ref, vc_ref, tab_ref, gu_ref, svn_ref, sga_ref, sgb_ref, ws_ref, bs_ref,
         wpa_ref, wpb_ref, wo_ref, lng_ref, lnb_ref, o_ref, oa_ref, ob_ref) = refs[4 + 2 * n_nbr:]
    else:
        (x_ref, mod_ref, qp_ref, kc_ref, vc_ref, gu_ref, svn_ref, sga_ref, sgb_ref,
         ws_ref, bs_ref, wpa_ref, wpb_ref, wo_ref, lng_ref, lnb_ref,
         o_ref, oa_ref, ob_ref) = refs

    lane = lax.broadcasted_iota(jnp.int32, (Q_BLOCK, LANES), 1)
    low_half = lane < HEAD_DIM
    first_step = pl.program_id(1) == 0
    last_step = pl.program_id(1) == pl.num_programs(1) - 1

    for sub in range(n_sub):
        rows = slice(sub * Q_BLOCK, (sub + 1) * Q_BLOCK)
        tabs = [TAB_LOW, TAB_MID, TAB_HIGH]
        if latent and sub == 0:
            tabs[0] = jnp.where(first_step, TAB_MASKED, tabs[0])
            tabs[2] = jnp.where(first_step, TAB_HIGH_FULL, tabs[2])
        if latent and sub == n_sub - 1:
            tabs[0] = jnp.where(last_step, TAB_LOW_FULL, tabs[0])
            tabs[2] = jnp.where(last_step, TAB_MASKED, tabs[2])
        for p in range(HEAD_PAIRS):
            cols = slice(p * LANES, (p + 1) * LANES)

            def per_head(q):
                return jnp.concatenate([jnp.where(low_half, q, 0), jnp.where(low_half, 0, q)], axis=0)

            s_parts = [_dot_nt(per_head(qp_ref[rows, cols]), kc_ref[:, cols])]
            if latent:
                q_lat = per_head(qr_ref[rows, cols])
                for m in range(3):
                    s = _dot_nt(q_lat, k_refs[sub + m][:, cols])
                    tab = tab_ref[tabs[m], 2 * p:2 * p + 2].reshape(2 * Q_BLOCK, Q_BLOCK)
                    s_parts.append(s + tab)
            mx = functools.reduce(jnp.maximum, s_parts).max(axis=-1, keepdims=True)
            p_parts = [jnp.exp(s - mx) for s in s_parts]
            denom = functools.reduce(jnp.add, p_parts).sum(axis=-1, keepdims=True)
            acc = _dot(p_parts[0].astype(BF16), vc_ref[:, cols])
            for m in range(len(p_parts) - 1):
                acc = acc + _dot(p_parts[m + 1].astype(BF16), v_refs[sub + m][:, cols])
            o = acc / denom
            oa_ref[rows, cols] = jnp.where(low_half, o[:Q_BLOCK], o[Q_BLOCK:]).astype(BF16)

    half = lax.broadcasted_iota(jnp.int32, (CHUNK, LANES), 1) < (SG_WIDTH // SG_GROUPS)
    for c2 in range(n_sub * Q_BLOCK // (2 * CHUNK)):
        chunk_rows = [slice((2 * c2 + i) * CHUNK, (2 * c2 + i + 1) * CHUNK) for i in range(2)]
        for p in range(SG_GROUPS // 2):
            cols = slice(p * LANES, (p + 1) * LANES)
            vl = jnp.concatenate([svn_ref[r, cols] for r in chunk_rows], axis=1)
            both = _dot(ws_ref[p], vl)
            for i, r in enumerate(chunk_rows):
                lanes = slice(i * LANES, (i + 1) * LANES)
                mixed = jnp.where(half, both[:CHUNK, lanes], both[CHUNK:, lanes]) + bs_ref[p]
                ob_ref[r, cols] = (gu_ref[r, cols].astype(F32) * mixed).astype(BF16)

    for sub in range(n_sub):
        rows = slice(sub * Q_BLOCK, (sub + 1) * Q_BLOCK)
        ya = _dot(oa_ref[rows, :], wpa_ref[...])
        yb = _dot(ob_ref[rows, :], wpb_ref[...])
        y = sga_ref[rows, :].astype(F32) * ya + sgb_ref[rows, :].astype(F32) * yb
        out = _dot(y.astype(BF16), wo_ref[...])
        r = ALPHA * x_ref[rows, :] + mod_ref[2:3, :] * out
        o_ref[rows, :] = _layer_norm(r, lng_ref[...], lnb_ref[...])


def _mix(x2d, mods, acts, ctx_kv, tabs, wts, *, layer, latent, n_sub):
    t = x2d.shape[0]
    seq = SEQ if latent else CTX_LEN
    tq = n_sub * Q_BLOCK
    steps = seq // tq
    blocks_per_seq = seq // Q_BLOCK
    tok = lambda width: pl.BlockSpec((tq, width), lambda b, j: (b * steps + j, 0))
    ctx_spec = pl.BlockSpec((CTX_LEN, NA_WIDTH), lambda b, j: (b, 0))
    w_specs = [_resident(a, layer) for a in wts]
    if latent:
        qr, qp, kr, v, gu, svn, sga, sgb = acts
        kc, vc = ctx_kv

        def nbr(m):
            return pl.BlockSpec(
                (Q_BLOCK, NA_WIDTH),
                lambda b, j: (b * blocks_per_seq
                              + jnp.clip(n_sub * j - 1 + m, 0, blocks_per_seq - 1), 0))

        nbrs = [nbr(m) for m in range(n_sub + 2)]
        in_specs = ([tok(D_MODEL), _mod_spec(layer, lambda b, j: b),
                     tok(NA_WIDTH), tok(NA_WIDTH)] + nbrs + nbrs
                    + [ctx_spec, ctx_spec, _resident(tabs, layer)]
                    + [tok(NA_WIDTH), tok(NA_WIDTH), tok(D_MODEL), tok(D_MODEL)] + w_specs)
        args = ([x2d, mods, qr, qp] + [kr] * (n_sub + 2) + [v] * (n_sub + 2)
                + [kc, vc, tabs, gu, svn, sga, sgb] + list(wts))
    else:
        qp, kc, vc, gu, svn, sga, sgb = acts
        in_specs = ([tok(D_MODEL), _mod_spec(layer, lambda b, j: CTX_MOD_ROW),
                     tok(NA_WIDTH), ctx_spec, ctx_spec]
                    + [tok(NA_WIDTH), tok(NA_WIDTH), tok(D_MODEL), tok(D_MODEL)] + w_specs)
        args = [x2d, mods, qp, kc, vc, gu, svn, sga, sgb] + list(wts)
    return pl.pallas_call(
        functools.partial(_mix_kernel, latent=latent, n_sub=n_sub),
        grid=(BATCH, steps),
        in_specs=in_specs,
        out_specs=tok(D_MODEL),
        out_shape=jax.ShapeDtypeStruct((t, D_MODEL), F32),
        scratch_shapes=[pltpu.VMEM((tq, NA_WIDTH), BF16),
                        pltpu.VMEM((tq, SG_WIDTH), BF16)],
        compiler_params=_params(2),
        name="mix_latent" if latent else "mix_ctx",
    )(*args)


def _ffn_kernel(x_ref, prev_ref, next_ref, mod_ref, wu_ref, cw_ref, cb_ref, wd_ref,
                lng_ref, lnb_ref, o_ref, h_ref, act_ref, f_ref, *z_refs, tm, seq):
    i = pl.program_id(0)
    if seq >= tm:
        tiles_per_seq = seq // tm
        has_prev = (i % tiles_per_seq != 0).astype(F32)
        has_next = (i % tiles_per_seq != tiles_per_seq - 1).astype(F32)
        inside = None
    else:
        has_prev = has_next = 0.0
        pos = lax.broadcasted_iota(jnp.int32, (tm, LANES), 0) % seq
        inside = ((pos != 0).astype(F32), (pos != seq - 1).astype(F32))
    shift = mod_ref[3:4, :]
    scale1 = 1.0 + mod_ref[4:5, :]
    rows = tm + 2 * HALO
    h_ref[0:tm, :] = (x_ref[...] * scale1 + shift).astype(BF16)
    h_ref[tm:rows, :] = jnp.concatenate(
        [(prev_ref[...] * scale1 + shift) * has_prev,
         (next_ref[...] * scale1 + shift) * has_next], axis=0).astype(BF16)

    def conv(lo, z_ref):
        z = _dot(h_ref[...], wu_ref[:, lo:lo + FF_CHUNK])
        outs = []
        for t in range(FF_CHUNK // LANES):
            cols = slice(lo + t * LANES, lo + (t + 1) * LANES)
            zt = z[:, t * LANES:(t + 1) * LANES]
            z_ref[t, 0:HALO, :] = zt[tm:tm + HALO]
            z_ref[t, HALO:HALO + tm, :] = zt[0:tm]
            z_ref[t, HALO + tm:rows, :] = zt[tm + HALO:rows]
            before = z_ref[t, HALO - 1:HALO - 1 + tm, :]
            after = z_ref[t, HALO + 1:HALO + 1 + tm, :]
            if inside is not None:
                before, after = before * inside[0], after * inside[1]
            outs.append(cw_ref[0:1, cols] * before + cw_ref[1:2, cols] * zt[0:tm]
                        + cw_ref[2:3, cols] * after + cb_ref[0:1, cols])
        return jnp.concatenate(outs, axis=1)

    def down(lo, width):
        return _dot(act_ref[:, lo:lo + width], wd_ref[lo:lo + width, :])

    assert N_FF_CHUNKS % 2 == 1
    for c in range(N_FF_CHUNKS):
        lo = c * FF_CHUNK
        zg_ref, za_ref = z_refs[2 * (c % 2)], z_refs[2 * (c % 2) + 1]
        act_ref[:, lo:lo + FF_CHUNK] = (jax.nn.silu(conv(D_FF + lo, zg_ref))
                                        * conv(lo, za_ref)).astype(BF16)
        if c == 1:
            f_ref[...] = down(0, 2 * FF_CHUNK)
        elif c % 2 == 1:
            f_ref[...] += down(lo - FF_CHUNK, 2 * FF_CHUNK)
    f = f_ref[...] + down(D_FF - FF_CHUNK, FF_CHUNK)
    r = ALPHA * x_ref[...] + mod_ref[5:6, :] * f
    o_ref[...] = _layer_norm(r, lng_ref[...], lnb_ref[...])


def _conv_ffn(x2d, mods, wts, *, layer, latent, tm, seq):
    t = x2d.shape[0]
    assert seq % tm == 0 or (tm % seq == 0 and not latent)
    n_halo_blocks = t // HALO
    per_tile = tm // HALO
    if latent:
        mod_spec = _mod_spec(layer, lambda i: i // (seq // tm))
    else:
        mod_spec = _mod_spec(layer, lambda i: CTX_MOD_ROW)
    tok = pl.BlockSpec((tm, D_MODEL), lambda i: (i, 0))
    prev = pl.BlockSpec((HALO, D_MODEL), lambda i: (jnp.maximum(i * per_tile - 1, 0), 0))
    nxt = pl.BlockSpec((HALO, D_MODEL),
                       lambda i: (jnp.minimum((i + 1) * per_tile, n_halo_blocks - 1), 0))
    return pl.pallas_call(
        functools.partial(_ffn_kernel, tm=tm, seq=seq),
        grid=(t // tm,),
        in_specs=[tok, prev, nxt, mod_spec] + [_resident(a, layer) for a in wts],
        out_specs=tok,
        out_shape=jax.ShapeDtypeStruct((t, D_MODEL), F32),
        scratch_shapes=[pltpu.VMEM((tm + 2 * HALO, D_MODEL), BF16),
                        pltpu.VMEM((tm, D_FF), BF16),
                        pltpu.VMEM((tm, D_MODEL), F32)]
                       + [pltpu.VMEM((FF_CHUNK // LANES, tm + 2 * HALO, LANES), F32)] * 4,
        compiler_params=_params(1),
        name="conv_ffn_latent" if latent else "conv_ffn_ctx",
    )(x2d, x2d, x2d, mods, *wts)


def _rope_tables():
    half = HEAD_DIM // 2
    nf = half // 2
    inv = ROPE_BASE ** (-jnp.arange(nf, dtype=F32) / nf)
    t = jnp.arange(SEQ)

    def part(pos):
        ang = pos.astype(F32)[:, None] * inv[None, :]
        cos = jnp.cos(ang)
        sin = jnp.sin(ang)
        return jnp.concatenate([cos, cos], axis=-1), jnp.concatenate([-sin, sin], axis=-1)

    cr, sr = part(t // GRID_W)
    cc, sc = part(t % GRID_W)
    cos = jnp.concatenate([cr, cc], axis=-1)
    sin = jnp.concatenate([sr, sc], axis=-1)
    reps = LANES // HEAD_DIM
    return jnp.tile(cos, (1, reps)), jnp.tile(sin, (1, reps))


def _row_ok(kind, qi, ki):
    if kind == TAB_MASKED:
        return False
    if kind in (TAB_LOW_FULL, TAB_MID, TAB_HIGH_FULL):
        return True
    return qi <= ki < qi + WIN_H


BIAS_HI = 32


def _bias_kernel(rp_ref, o_ref, pt_ref):
    n_pairs = 2 * WIN_H - 2
    qc = lax.broadcasted_iota(jnp.int32, (GRID_W, LANES), 0)
    lane = lax.broadcasted_iota(jnp.int32, (GRID_W, LANES), 1)
    upper = lane >= GRID_W
    kc = jnp.where(upper, lane - GRID_W, lane)
    win_start = jnp.clip(qc - WIN_W // 2, 0, GRID_W - WIN_W)
    col_ok = (kc >= win_start) & (kc < win_start + WIN_W)
    idx = jnp.clip(kc - qc + WIN_W - 1, 0, 2 * WIN_W - 2) + jnp.where(upper, BIAS_HI, 0)
    masked = jnp.full((GRID_W, LANES), NEG_INF, F32)
    for dy in range(n_pairs):
        src = jnp.broadcast_to(rp_ref[dy:dy + 1, :], (GRID_W, LANES))
        pt_ref[dy] = jnp.where(col_ok, jnp.take_along_axis(src, idx, axis=1), NEG_INF)
    for kind in range(N_TAB_KINDS):
        for qi in range(Q_ROWS):
            for kp in range(Q_ROWS // 2):
                ki = Q_ROWS * TAB_KEY_BLOCK[kind] + 2 * kp
                ok = (_row_ok(kind, qi, ki), _row_ok(kind, qi, ki + 1))
                tile = pt_ref[ki - qi + WIN_H // 2 - 1] if any(ok) else masked
                if ok == (True, False):
                    tile = jnp.where(upper, NEG_INF, tile)
                elif ok == (False, True):
                    tile = jnp.where(upper, tile, NEG_INF)
                o_ref[kind, qi * GRID_W:(qi + 1) * GRID_W, kp * LANES:(kp + 1) * LANES] = tile


def _bias_tables(rpb):
    n_dx = 2 * WIN_W - 1
    assert n_dx <= BIAS_HI and BIAS_HI + n_dx <= LANES
    pad = lambda a, n: jnp.pad(a, ((0, 0), (0, 0), (0, 2), (0, n - a.shape[-1])))
    rp = jnp.concatenate([pad(rpb[:, :, :-1], BIAS_HI), pad(rpb[:, :, 1:], LANES - BIAS_HI)], axis=-1)
    rows = rp.shape[2]
    return pl.pallas_call(
        _bias_kernel,
        grid=(DEPTH, NA_HEADS),
        in_specs=[pl.BlockSpec((None, None, rows, LANES), lambda l, h: (l, h, 0, 0))],
        out_specs=pl.BlockSpec((None, N_TAB_KINDS, None, Q_BLOCK, Q_BLOCK),
                               lambda l, h: (l, 0, h, 0, 0)),
        out_shape=jax.ShapeDtypeStruct((DEPTH, N_TAB_KINDS, NA_HEADS, Q_BLOCK, Q_BLOCK), F32),
        scratch_shapes=[pltpu.VMEM((2 * WIN_H - 2, GRID_W, LANES), F32)],
        compiler_params=_params(2),
        name="bias_tables",
    )(rp)


def _stacked_weights(w_in, sg_ln_g, sg_ln_b, w_s, b_s, w_pa, w_pb, w_o, ln1_g, ln1_b,
                     w_up, conv_w, conv_b, w_down, ln2_g, ln2_b):
    row = lambda a: a.reshape(DEPTH, 1, -1)
    bf16 = lambda a: a.astype(BF16)
    bs = jnp.repeat(b_s.reshape(DEPTH, SG_GROUPS // 2, 2, CHUNK).transpose(0, 1, 3, 2),
                    SG_WIDTH // SG_GROUPS, axis=-1)
    ws = bf16(w_s).reshape(DEPTH, SG_GROUPS // 2, 2 * CHUNK, CHUNK)
    inproj = (bf16(w_in), row(sg_ln_g), row(sg_ln_b))
    mix = (ws, bs, bf16(w_pa), bf16(w_pb), bf16(w_o), row(ln1_g), row(ln1_b))
    ffn = (bf16(w_up), conv_w, row(conv_b), bf16(w_down), row(ln2_g), row(ln2_b))
    return inproj, mix, ffn


def kernel(x, c, ctx, c_ctx, w_ada, b_ada, w_in, rpb, sg_ln_g, sg_ln_b, w_s, b_s, w_pa, w_pb, w_o,
           ln1_g, ln1_b, w_up, conv_w, conv_b, w_down, ln2_g, ln2_b):
    assert x.shape == (BATCH, SEQ, D_MODEL) and ctx.shape == (BATCH, CTX_LEN, D_MODEL)
    cc = jnp.concatenate([c, c_ctx[None, :], jnp.zeros((MOD_ROWS - BATCH - 1, D_MODEL), F32)], axis=0)
    mods = _modulation(cc, w_ada, b_ada)
    rope = _rope_tables()
    tabs = _bias_tables(rpb)
    (w_in_b, lng, lnb), mix_w, ffn_w = _stacked_weights(
        w_in, sg_ln_g, sg_ln_b, w_s, b_s, w_pa, w_pb, w_o, ln1_g, ln1_b,
        w_up, conv_w, conv_b, w_down, ln2_g, ln2_b)
    xl = x.reshape(BATCH * SEQ, D_MODEL)
    xc = ctx.reshape(BATCH * CTX_LEN, D_MODEL)
    ctx_tiles = dict(tm=2 * CTX_LEN, seq=2 * CTX_LEN)
    for i in range(DEPTH):
        if i < DEPTH - 1:
            c_acts = _in_projection(xc, mods, w_in_b, lng, lnb, None, layer=i, mode="ctx",
                                    **ctx_tiles)
            ctx_kv = (c_acts[1], c_acts[2])
        else:
            ctx_kv = _in_projection(xc, mods, w_in_b, None, None, None, layer=i, mode="ctx_kv",
                                    **ctx_tiles)
        acts = _in_projection(xl, mods, w_in_b, lng, lnb, rope, layer=i, mode="latent", tm=1024,
                              seq=SEQ)
        xl = _mix(xl, mods, acts, ctx_kv, tabs, mix_w, layer=i, latent=True, n_sub=2)
        xl = _conv_ffn(xl, mods, ffn_w, layer=i, latent=True, tm=512, seq=SEQ)
        if i < DEPTH - 1:
            xc = _mix(xc, mods, c_acts, None, None, mix_w, layer=i, latent=False, n_sub=1)
            xc = _conv_ffn(xc, mods, ffn_w, layer=i, latent=False, tm=2 * CTX_LEN, seq=CTX_LEN)
    return xl.reshape(BATCH, SEQ, D_MODEL)
```

```python
import functools

import jax
import jax.numpy as jnp
from jax import lax
from jax.experimental import pallas as pl
from jax.experimental.pallas import tpu as pltpu

D_MODEL = 1024
BATCH = 4
SEQ = 4096
DEPTH = 2
GRID_W = 64
CTX_LEN = 256
NA_HEADS = 8
HEAD_DIM = 64
NA_WIDTH = NA_HEADS * HEAD_DIM
WIN_H = 8
WIN_W = 16
ROPE_BASE = 10000.0
SG_GROUPS = 8
SG_WIDTH = 512
CHUNK = 128
D_FF = 2816
ALPHA = (2 * DEPTH) ** 0.25
LN_EPS = 1e-5
NEG_INF = -1e30

F32 = jnp.float32
BF16 = jnp.bfloat16

LANES = 128
HEAD_PAIRS = NA_HEADS // 2
MOD_ROWS = 8
CTX_MOD_ROW = BATCH
Q_ROWS = 4
Q_BLOCK = Q_ROWS * GRID_W
K_ROWS = 3 * Q_ROWS
TAB_LOW, TAB_LOW_FULL, TAB_MID, TAB_HIGH, TAB_HIGH_FULL, TAB_MASKED = range(6)
TAB_KEY_BLOCK = (0, 0, 1, 2, 2, 0)
N_TAB_KINDS = len(TAB_KEY_BLOCK)
FF_CHUNK = 256
N_FF_CHUNKS = D_FF // FF_CHUNK
HALO = 8
VMEM_LIMIT = 56 * 1024 * 1024


def _dot(a, b):
    return jnp.dot(a, b, preferred_element_type=F32)


def _dot_nt(a, b):
    return lax.dot_general(a, b, (((1,), (1,)), ((), ())), preferred_element_type=F32)


def _layer_norm(v, g, b):
    mu = jnp.mean(v, axis=-1, keepdims=True)
    d = v - mu
    var = jnp.mean(d * d, axis=-1, keepdims=True)
    return d * lax.rsqrt(var + LN_EPS) * g + b


def _params(n_axes):
    return pltpu.CompilerParams(dimension_semantics=("arbitrary",) * n_axes,
                                vmem_limit_bytes=VMEM_LIMIT)


def _resident(stacked, layer, block=None, index=None):
    block = tuple(stacked.shape[1:]) if block is None else block
    index = (0,) * len(block) if index is None else index
    return pl.BlockSpec((None,) + block, lambda *_: (layer,) + index,
                        pipeline_mode=pl.Buffered(1))


def _mod_spec(layer, row_map):
    return pl.BlockSpec((None, None, 6, D_MODEL), lambda *g: (layer, row_map(*g), 0, 0))


def _mod_kernel(cc_ref, w_ref, b_ref, o_ref):
    s = jax.nn.silu(cc_ref[...]).astype(BF16)
    o_ref[...] = _dot(s, w_ref[...].astype(BF16)) + b_ref[...]


def _modulation(cc, w_ada, b_ada):
    tn = 1536
    n6 = 6 * D_MODEL
    out = pl.pallas_call(
        _mod_kernel,
        grid=(DEPTH, n6 // tn),
        in_specs=[pl.BlockSpec((MOD_ROWS, D_MODEL), lambda l, n: (0, 0)),
                  pl.BlockSpec((None, D_MODEL, tn), lambda l, n: (l, 0, n)),
                  pl.BlockSpec((None, 1, tn), lambda l, n: (l, 0, n))],
        out_specs=pl.BlockSpec((None, MOD_ROWS, tn), lambda l, n: (l, 0, n)),
        out_shape=jax.ShapeDtypeStruct((DEPTH, MOD_ROWS, n6), F32),
        compiler_params=_params(2),
        name="adaln_modulation",
    )(cc, w_ada, b_ada.reshape(DEPTH, 1, n6))
    return out.reshape(DEPTH, MOD_ROWS, 6, D_MODEL)


def _rope_store(z, cos, sin, first_half, out_ref, scale):
    for j in range(NA_WIDTH // LANES):
        zj = z[:, j * LANES:(j + 1) * LANES]
        partner = jnp.where(first_half, pltpu.roll(zj, LANES - 16, 1), pltpu.roll(zj, 16, 1))
        r = zj * cos + partner * sin
        if scale != 1.0:
            r = r * scale
        out_ref[:, j * LANES:(j + 1) * LANES] = r.astype(out_ref.dtype)


def _inproj_kernel(*refs, mode):
    if mode == "latent":
        (x_ref, mod_ref, w_ref, lng_ref, lnb_ref, cos_ref, sin_ref,
         qr_ref, qp_ref, kr_ref, v_ref, gu_ref, svn_ref, sga_ref, sgb_ref) = refs
    elif mode == "ctx":
        (x_ref, mod_ref, w_ref, lng_ref, lnb_ref,
         qp_ref, kr_ref, v_ref, gu_ref, svn_ref, sga_ref, sgb_ref) = refs
    else:
        x_ref, mod_ref, wk_ref, wv_ref, kr_ref, v_ref = refs

    scale = HEAD_DIM ** -0.5
    h = (x_ref[...] * (1.0 + mod_ref[1:2, :]) + mod_ref[0:1, :]).astype(BF16)

    if mode == "ctx_kv":
        kr_ref[...] = _dot(h, wk_ref[...]).astype(BF16)
        v_ref[...] = _dot(h, wv_ref[...]).astype(BF16)
        return

    def proj(lo, width):
        return _dot(h, w_ref[:, lo:lo + width])

    zq = proj(0, NA_WIDTH)
    zk = proj(NA_WIDTH, NA_WIDTH)
    qp_ref[...] = (zq * scale).astype(BF16)
    if mode == "latent":
        cos = cos_ref[...]
        sin = sin_ref[...]
        lane = lax.broadcasted_iota(jnp.int32, cos.shape, 1)
        first_half = (lane % 32) < 16
        _rope_store(zq, cos, sin, first_half, qr_ref, scale)
        _rope_store(zk, cos, sin, first_half, kr_ref, 1.0)
    else:
        kr_ref[...] = zk.astype(BF16)
    v_ref[...] = proj(2 * NA_WIDTH, NA_WIDTH).astype(BF16)
    gu_ref[...] = jax.nn.gelu(proj(3 * NA_WIDTH, SG_WIDTH)).astype(BF16)
    sv = jax.nn.gelu(proj(3 * NA_WIDTH + SG_WIDTH, SG_WIDTH))
    svn_ref[...] = _layer_norm(sv, lng_ref[...], lnb_ref[...]).astype(BF16)
    lo = 3 * NA_WIDTH + 2 * SG_WIDTH
    sga_ref[...] = jax.nn.sigmoid(proj(lo, D_MODEL)).astype(BF16)
    sgb_ref[...] = jax.nn.sigmoid(proj(lo + D_MODEL, D_MODEL)).astype(BF16)


def _in_projection(x2d, mods, w, lng, lnb, rope, *, layer, mode, tm, seq):
    t = x2d.shape[0]
    tiles_per_seq = seq // tm
    if mode == "latent":
        mod_spec = _mod_spec(layer, lambda i: i // tiles_per_seq)
    else:
        mod_spec = _mod_spec(layer, lambda i: CTX_MOD_ROW)
    tok = lambda width: pl.BlockSpec((tm, width), lambda i: (i, 0))
    if mode == "ctx_kv":
        in_specs = [tok(D_MODEL), mod_spec,
                    _resident(w, layer, (D_MODEL, NA_WIDTH), (0, 1)),
                    _resident(w, layer, (D_MODEL, NA_WIDTH), (0, 2))]
        args = [x2d, mods, w, w]
    else:
        in_specs = [tok(D_MODEL), mod_spec, _resident(w, layer),
                    _resident(lng, layer), _resident(lnb, layer)]
        args = [x2d, mods, w, lng, lnb]
    if mode == "latent":
        pos = pl.BlockSpec((tm, LANES), lambda i: (i % tiles_per_seq, 0))
        in_specs += [pos, pos]
        args += list(rope)
    widths = {"latent": [NA_WIDTH] * 6 + [D_MODEL] * 2,
              "ctx": [NA_WIDTH] * 5 + [D_MODEL] * 2,
              "ctx_kv": [NA_WIDTH] * 2}[mode]
    return pl.pallas_call(
        functools.partial(_inproj_kernel, mode=mode),
        grid=(t // tm,),
        in_specs=in_specs,
        out_specs=[tok(wd) for wd in widths],
        out_shape=[jax.ShapeDtypeStruct((t, wd), BF16) for wd in widths],
        compiler_params=_params(1),
        name="in_projection_" + mode,
    )(*args)


def _mix_kernel(*refs, latent, n_sub):
    n_nbr = n_sub + 2
    if latent:
        x_ref, mod_ref, qr_ref, qp_ref = refs[:4]
        k_refs = refs[4:4 + n_nbr]
        v_refs = refs[4 + n_nbr:4 + 2 * n_nbr]
        (kc_ref, vc_ref, tab_ref, gu_ref, svn_ref, sga_ref, sgb_ref, ws_ref, bs_ref,
         wpa_ref, wpb_ref, wo_ref, lng_ref, lnb_ref, o_ref, oa_ref, ob_ref) = refs[4 + 2 * n_nbr:]
    else:
        (x_ref, mod_ref, qp_ref, kc_ref, vc_ref, gu_ref, svn_ref, sga_ref, sgb_ref,
         ws_ref, bs_ref, wpa_ref, wpb_ref, wo_ref, lng_ref, lnb_ref,
         o_ref, oa_ref, ob_ref) = refs

    lane = lax.broadcasted_iota(jnp.int32, (Q_BLOCK, LANES), 1)
    low_half = lane < HEAD_DIM
    first_step = pl.program_id(1) == 0
    last_step = pl.program_id(1) == pl.num_programs(1) - 1

    for sub in range(n_sub):
        rows = slice(sub * Q_BLOCK, (sub + 1) * Q_BLOCK)
        tabs = [TAB_LOW, TAB_MID, TAB_HIGH]
        if latent and sub == 0:
            tabs[0] = jnp.where(first_step, TAB_MASKED, tabs[0])
            tabs[2] = jnp.where(first_step, TAB_HIGH_FULL, tabs[2])
        if latent and sub == n_sub - 1:
            tabs[0] = jnp.where(last_step, TAB_LOW_FULL, tabs[0])
            tabs[2] = jnp.where(last_step, TAB_MASKED, tabs[2])
        for p in range(HEAD_PAIRS):
            cols = slice(p * LANES, (p + 1) * LANES)

            def per_head(q):
                return jnp.concatenate([jnp.where(low_half, q, 0), jnp.where(low_half, 0, q)], axis=0)

            s_parts = [_dot_nt(per_head(qp_ref[rows, cols]), kc_ref[:, cols])]
            if latent:
                q_lat = per_head(qr_ref[rows, cols])
                for m in range(3):
                    s = _dot_nt(q_lat, k_refs[sub + m][:, cols])
                    tab = tab_ref[tabs[m], 2 * p:2 * p + 2].reshape(2 * Q_BLOCK, Q_BLOCK)
                    s_parts.append(s + tab)
            mx = functools.reduce(jnp.maximum, [s.max(axis=-1, keepdims=True) for s in s_parts])
            p_parts = [jnp.exp(s - mx) for s in s_parts]
            denom = functools.reduce(jnp.add, [pp.sum(axis=-1, keepdims=True) for pp in p_parts])
            acc = _dot(p_parts[0].astype(BF16), vc_ref[:, cols])
            for m in range(len(p_parts) - 1):
                acc = acc + _dot(p_parts[m + 1].astype(BF16), v_refs[sub + m][:, cols])
            o = acc / denom
            oa_ref[rows, cols] = jnp.where(low_half, o[:Q_BLOCK], o[Q_BLOCK:]).astype(BF16)

    half = lax.broadcasted_iota(jnp.int32, (CHUNK, LANES), 1) < (SG_WIDTH // SG_GROUPS)
    for c2 in range(n_sub * Q_BLOCK // (2 * CHUNK)):
        chunk_rows = [slice((2 * c2 + i) * CHUNK, (2 * c2 + i + 1) * CHUNK) for i in range(2)]
        for p in range(SG_GROUPS // 2):
            cols = slice(p * LANES, (p + 1) * LANES)
            vl = jnp.concatenate([svn_ref[r, cols] for r in chunk_rows], axis=1)
            both = _dot(ws_ref[p], vl)
            for i, r in enumerate(chunk_rows):
                lanes = slice(i * LANES, (i + 1) * LANES)
                mixed = jnp.where(half, both[:CHUNK, lanes], both[CHUNK:, lanes]) + bs_ref[p]
                ob_ref[r, cols] = (gu_ref[r, cols].astype(F32) * mixed).astype(BF16)

    for sub in range(n_sub):
        rows = slice(sub * Q_BLOCK, (sub + 1) * Q_BLOCK)
        ya = _dot(oa_ref[rows, :], wpa_ref[...])
        yb = _dot(ob_ref[rows, :], wpb_ref[...])
        y = sga_ref[rows, :].astype(F32) * ya + sgb_ref[rows, :].astype(F32) * yb
        out = _dot(y.astype(BF16), wo_ref[...])
        r = ALPHA * x_ref[rows, :] + mod_ref[2:3, :] * out
        o_ref[rows, :] = _layer_norm(r, lng_ref[...], lnb_ref[...])


def _mix(x2d, mods, acts, ctx_kv, tabs, wts, *, layer, latent, n_sub):
    t = x2d.shape[0]
    seq = SEQ if latent else CTX_LEN
    tq = n_sub * Q_BLOCK
    steps = seq // tq
    blocks_per_seq = seq // Q_BLOCK
    tok = lambda width: pl.BlockSpec((tq, width), lambda b, j: (b * steps + j, 0))
    ctx_spec = pl.BlockSpec((CTX_LEN, NA_WIDTH), lambda b, j: (b, 0))
    w_specs = [_resident(a, layer) for a in wts]
    if latent:
        qr, qp, kr, v, gu, svn, sga, sgb = acts
        kc, vc = ctx_kv

        def nbr(m):
            return pl.BlockSpec(
                (Q_BLOCK, NA_WIDTH),
                lambda b, j: (b * blocks_per_seq
                              + jnp.clip(n_sub * j - 1 + m, 0, blocks_per_seq - 1), 0))

        nbrs = [nbr(m) for m in range(n_sub + 2)]
        in_specs = ([tok(D_MODEL), _mod_spec(layer, lambda b, j: b),
                     tok(NA_WIDTH), tok(NA_WIDTH)] + nbrs + nbrs
                    + [ctx_spec, ctx_spec, _resident(tabs, layer)]
                    + [tok(NA_WIDTH), tok(NA_WIDTH), tok(D_MODEL), tok(D_MODEL)] + w_specs)
        args = ([x2d, mods, qr, qp] + [kr] * (n_sub + 2) + [v] * (n_sub + 2)
                + [kc, vc, tabs, gu, svn, sga, sgb] + list(wts))
    else:
        qp, kc, vc, gu, svn, sga, sgb = acts
        in_specs = ([tok(D_MODEL), _mod_spec(layer, lambda b, j: CTX_MOD_ROW),
                     tok(NA_WIDTH), ctx_spec, ctx_spec]
                    + [tok(NA_WIDTH), tok(NA_WIDTH), tok(D_MODEL), tok(D_MODEL)] + w_specs)
        args = [x2d, mods, qp, kc, vc, gu, svn, sga, sgb] + list(wts)
    return pl.pallas_call(
        functools.partial(_mix_kernel, latent=latent, n_sub=n_sub),
        grid=(BATCH, steps),
        in_specs=in_specs,
        out_specs=tok(D_MODEL),
        out_shape=jax.ShapeDtypeStruct((t, D_MODEL), F32),
        scratch_shapes=[pltpu.VMEM((tq, NA_WIDTH), BF16),
                        pltpu.VMEM((tq, SG_WIDTH), BF16)],
        compiler_params=_params(2),
        name="mix_latent" if latent else "mix_ctx",
    )(*args)


def _ffn_kernel(x_ref, prev_ref, next_ref, mod_ref, wu_ref, cw_ref, cb_ref, wd_ref,
                lng_ref, lnb_ref, o_ref, h_ref, act_ref, f_ref, *z_refs, tm, seq):
    i = pl.program_id(0)
    if seq >= tm:
        tiles_per_seq = seq // tm
        has_prev = (i % tiles_per_seq != 0).astype(F32)
        has_next = (i % tiles_per_seq != tiles_per_seq - 1).astype(F32)
        inside = None
    else:
        has_prev = has_next = 0.0
        pos = lax.broadcasted_iota(jnp.int32, (tm, LANES), 0) % seq
        inside = ((pos != 0).astype(F32), (pos != seq - 1).astype(F32))
    shift = mod_ref[3:4, :]
    scale1 = 1.0 + mod_ref[4:5, :]
    rows = tm + 2 * HALO
    h_ref[0:tm, :] = (x_ref[...] * scale1 + shift).astype(BF16)
    h_ref[tm:rows, :] = jnp.concatenate(
        [(prev_ref[...] * scale1 + shift) * has_prev,
         (next_ref[...] * scale1 + shift) * has_next], axis=0).astype(BF16)

    def conv(lo, z_ref):
        z = _dot(h_ref[...], wu_ref[:, lo:lo + FF_CHUNK])
        outs = []
        for t in range(FF_CHUNK // LANES):
            cols = slice(lo + t * LANES, lo + (t + 1) * LANES)
            zt = z[:, t * LANES:(t + 1) * LANES]
            z_ref[t, 0:HALO, :] = zt[tm:tm + HALO]
            z_ref[t, HALO:HALO + tm, :] = zt[0:tm]
            z_ref[t, HALO + tm:rows, :] = zt[tm + HALO:rows]
            before = z_ref[t, HALO - 1:HALO - 1 + tm, :]
            after = z_ref[t, HALO + 1:HALO + 1 + tm, :]
            if inside is not None:
                before, after = before * inside[0], after * inside[1]
            outs.append(cw_ref[0:1, cols] * before + cw_ref[1:2, cols] * zt[0:tm]
                        + cw_ref[2:3, cols] * after + cb_ref[0:1, cols])
        return jnp.concatenate(outs, axis=1)

    def down(lo, width):
        return _dot(act_ref[:, lo:lo + width], wd_ref[lo:lo + width, :])

    assert N_FF_CHUNKS % 2 == 1
    for c in range(N_FF_CHUNKS):
        lo = c * FF_CHUNK
        zg_ref, za_ref = z_refs[2 * (c % 2)], z_refs[2 * (c % 2) + 1]
        act_ref[:, lo:lo + FF_CHUNK] = (jax.nn.silu(conv(D_FF + lo, zg_ref))
                                        * conv(lo, za_ref)).astype(BF16)
        if c == 1:
            f_ref[...] = down(0, 2 * FF_CHUNK)
        elif c % 2 == 1:
            f_ref[...] += down(lo - FF_CHUNK, 2 * FF_CHUNK)
    f = f_ref[...] + down(D_FF - FF_CHUNK, FF_CHUNK)
    r = ALPHA * x_ref[...] + mod_ref[5:6, :] * f
    o_ref[...] = _layer_norm(r, lng_ref[...], lnb_ref[...])


def _conv_ffn(x2d, mods, wts, *, layer, latent, tm, seq):
    t = x2d.shape[0]
    assert seq % tm == 0 or (tm % seq == 0 and not latent)
    n_halo_blocks = t // HALO
    per_tile = tm // HALO
    if latent:
        mod_spec = _mod_spec(layer, lambda i: i // (seq // tm))
    else:
        mod_spec = _mod_spec(layer, lambda i: CTX_MOD_ROW)
    tok = pl.BlockSpec((tm, D_MODEL), lambda i: (i, 0))
    prev = pl.BlockSpec((HALO, D_MODEL), lambda i: (jnp.maximum(i * per_tile - 1, 0), 0))
    nxt = pl.BlockSpec((HALO, D_MODEL),
                       lambda i: (jnp.minimum((i + 1) * per_tile, n_halo_blocks - 1), 0))
    return pl.pallas_call(
        functools.partial(_ffn_kernel, tm=tm, seq=seq),
        grid=(t // tm,),
        in_specs=[tok, prev, nxt, mod_spec] + [_resident(a, layer) for a in wts],
        out_specs=tok,
        out_shape=jax.ShapeDtypeStruct((t, D_MODEL), F32),
        scratch_shapes=[pltpu.VMEM((tm + 2 * HALO, D_MODEL), BF16),
                        pltpu.VMEM((tm, D_FF), BF16),
                        pltpu.VMEM((tm, D_MODEL), F32)]
                       + [pltpu.VMEM((FF_CHUNK // LANES, tm + 2 * HALO, LANES), F32)] * 4,
        compiler_params=_params(1),
        name="conv_ffn_latent" if latent else "conv_ffn_ctx",
    )(x2d, x2d, x2d, mods, *wts)


def _rope_tables():
    half = HEAD_DIM // 2
    nf = half // 2
    inv = ROPE_BASE ** (-jnp.arange(nf, dtype=F32) / nf)
    t = jnp.arange(SEQ)

    def part(pos):
        ang = pos.astype(F32)[:, None] * inv[None, :]
        cos = jnp.cos(ang)
        sin = jnp.sin(ang)
        return jnp.concatenate([cos, cos], axis=-1), jnp.concatenate([-sin, sin], axis=-1)

    cr, sr = part(t // GRID_W)
    cc, sc = part(t % GRID_W)
    cos = jnp.concatenate([cr, cc], axis=-1)
    sin = jnp.concatenate([sr, sc], axis=-1)
    reps = LANES // HEAD_DIM
    return jnp.tile(cos, (1, reps)), jnp.tile(sin, (1, reps))


def _row_ok(kind, qi, ki):
    if kind == TAB_MASKED:
        return False
    if kind in (TAB_LOW_FULL, TAB_MID, TAB_HIGH_FULL):
        return True
    return qi <= ki < qi + WIN_H


BIAS_HI = 32


def _bias_kernel(rp_ref, o_ref, pt_ref):
    n_pairs = 2 * WIN_H - 2
    qc = lax.broadcasted_iota(jnp.int32, (GRID_W, LANES), 0)
    lane = lax.broadcasted_iota(jnp.int32, (GRID_W, LANES), 1)
    upper = lane >= GRID_W
    kc = jnp.where(upper, lane - GRID_W, lane)
    win_start = jnp.clip(qc - WIN_W // 2, 0, GRID_W - WIN_W)
    col_ok = (kc >= win_start) & (kc < win_start + WIN_W)
    idx = jnp.clip(kc - qc + WIN_W - 1, 0, 2 * WIN_W - 2) + jnp.where(upper, BIAS_HI, 0)
    masked = jnp.full((GRID_W, LANES), NEG_INF, F32)
    for dy in range(n_pairs):
        src = jnp.broadcast_to(rp_ref[dy:dy + 1, :], (GRID_W, LANES))
        pt_ref[dy] = jnp.where(col_ok, jnp.take_along_axis(src, idx, axis=1), NEG_INF)
    for kind in range(N_TAB_KINDS):
        for qi in range(Q_ROWS):
            for kp in range(Q_ROWS // 2):
                ki = Q_ROWS * TAB_KEY_BLOCK[kind] + 2 * kp
                ok = (_row_ok(kind, qi, ki), _row_ok(kind, qi, ki + 1))
                tile = pt_ref[ki - qi + WIN_H // 2 - 1] if any(ok) else masked
                if ok == (True, False):
                    tile = jnp.where(upper, NEG_INF, tile)
                elif ok == (False, True):
                    tile = jnp.where(upper, tile, NEG_INF)
                o_ref[kind, qi * GRID_W:(qi + 1) * GRID_W, kp * LANES:(kp + 1) * LANES] = tile


def _bias_tables(rpb):
    n_dx = 2 * WIN_W - 1
    assert n_dx <= BIAS_HI and BIAS_HI + n_dx <= LANES
    pad = lambda a, n: jnp.pad(a, ((0, 0), (0, 0), (0, 2), (0, n - a.shape[-1])))
    rp = jnp.concatenate([pad(rpb[:, :, :-1], BIAS_HI), pad(rpb[:, :, 1:], LANES - BIAS_HI)], axis=-1)
    rows = rp.shape[2]
    return pl.pallas_call(
        _bias_kernel,
        grid=(DEPTH, NA_HEADS),
        in_specs=[pl.BlockSpec((None, None, rows, LANES), lambda l, h: (l, h, 0, 0))],
        out_specs=pl.BlockSpec((None, N_TAB_KINDS, None, Q_BLOCK, Q_BLOCK),
                               lambda l, h: (l, 0, h, 0, 0)),
        out_shape=jax.ShapeDtypeStruct((DEPTH, N_TAB_KINDS, NA_HEADS, Q_BLOCK, Q_BLOCK), F32),
        scratch_shapes=[pltpu.VMEM((2 * WIN_H - 2, GRID_W, LANES), F32)],
        compiler_params=_params(2),
        name="bias_tables",
    )(rp)


def _stacked_weights(w_in, sg_ln_g, sg_ln_b, w_s, b_s, w_pa, w_pb, w_o, ln1_g, ln1_b,
                     w_up, conv_w, conv_b, w_down, ln2_g, ln2_b):
    row = lambda a: a.reshape(DEPTH, 1, -1)
    bf16 = lambda a: a.astype(BF16)
    bs = jnp.repeat(b_s.reshape(DEPTH, SG_GROUPS // 2, 2, CHUNK).transpose(0, 1, 3, 2),
                    SG_WIDTH // SG_GROUPS, axis=-1)
    ws = bf16(w_s).reshape(DEPTH, SG_GROUPS // 2, 2 * CHUNK, CHUNK)
    inproj = (bf16(w_in), row(sg_ln_g), row(sg_ln_b))
    mix = (ws, bs, bf16(w_pa), bf16(w_pb), bf16(w_o), row(ln1_g), row(ln1_b))
    ffn = (bf16(w_up), conv_w, row(conv_b), bf16(w_down), row(ln2_g), row(ln2_b))
    return inproj, mix, ffn


def kernel(x, c, ctx, c_ctx, w_ada, b_ada, w_in, rpb, sg_ln_g, sg_ln_b, w_s, b_s, w_pa, w_pb, w_o,
           ln1_g, ln1_b, w_up, conv_w, conv_b, w_down, ln2_g, ln2_b):
    assert x.shape == (BATCH, SEQ, D_MODEL) and ctx.shape == (BATCH, CTX_LEN, D_MODEL)
    cc = jnp.concatenate([c, c_ctx[None, :], jnp.zeros((MOD_ROWS - BATCH - 1, D_MODEL), F32)], axis=0)
    mods = _modulation(cc, w_ada, b_ada)
    rope = _rope_tables()
    tabs = _bias_tables(rpb)
    (w_in_b, lng, lnb), mix_w, ffn_w = _stacked_weights(
        w_in, sg_ln_g, sg_ln_b, w_s, b_s, w_pa, w_pb, w_o, ln1_g, ln1_b,
        w_up, conv_w, conv_b, w_down, ln2_g, ln2_b)
    xl = x.reshape(BATCH * SEQ, D_MODEL)
    xc = ctx.reshape(BATCH * CTX_LEN, D_MODEL)
    ctx_tiles = dict(tm=2 * CTX_LEN, seq=2 * CTX_LEN)
    for i in range(DEPTH):
        if i < DEPTH - 1:
            c_acts = _in_projection(xc, mods, w_in_b, lng, lnb, None, layer=i, mode="ctx",
                                    **ctx_tiles)
            ctx_kv = (c_acts[1], c_acts[2])
        else:
            ctx_kv = _in_projection(xc, mods, w_in_b, None, None, None, layer=i, mode="ctx_kv",
                                    **ctx_tiles)
        acts = _in_projection(xl, mods, w_in_b, lng, lnb, rope, layer=i, mode="latent", tm=1024,
                              seq=SEQ)
        xl = _mix(xl, mods, acts, ctx_kv, tabs, mix_w, layer=i, latent=True, n_sub=2)
        xl = _conv_ffn(xl, mods, ffn_w, layer=i, latent=True, tm=512, seq=SEQ)
        if i < DEPTH - 1:
            xc = _mix(xc, mods, c_acts, None, None, mix_w, layer=i, latent=False, n_sub=1)
            xc = _conv_ffn(xc, mods, ffn_w, layer=i, latent=False, tm=2 * CTX_LEN, seq=CTX_LEN)
    return xl.reshape(BATCH, SEQ, D_MODEL)
```

```python
import functools

import jax
import jax.numpy as jnp
from jax import lax
from jax.experimental import pallas as pl
from jax.experimental.pallas import tpu as pltpu

D_MODEL = 1024
BATCH = 4
SEQ = 4096
DEPTH = 2
GRID_W = 64
CTX_LEN = 256
NA_HEADS = 8
HEAD_DIM = 64
NA_WIDTH = NA_HEADS * HEAD_DIM
WIN_H = 8
WIN_W = 16
ROPE_BASE = 10000.0
SG_GROUPS = 8
SG_WIDTH = 512
CHUNK = 128
D_FF = 2816
ALPHA = (2 * DEPTH) ** 0.25
LN_EPS = 1e-5
NEG_INF = -1e30

F32 = jnp.float32
BF16 = jnp.bfloat16

LANES = 128
HEAD_PAIRS = NA_HEADS // 2
MOD_ROWS = 8
CTX_MOD_ROW = BATCH
Q_ROWS = 4
Q_BLOCK = Q_ROWS * GRID_W
K_ROWS = 3 * Q_ROWS
TAB_LOW, TAB_LOW_FULL, TAB_MID, TAB_HIGH, TAB_HIGH_FULL, TAB_MASKED = range(6)
TAB_KEY_BLOCK = (0, 0, 1, 2, 2, 0)
N_TAB_KINDS = len(TAB_KEY_BLOCK)
FF_CHUNK = 256
N_FF_CHUNKS = D_FF // FF_CHUNK
HALO = 8
VMEM_LIMIT = 56 * 1024 * 1024


def _dot(a, b):
    return jnp.dot(a, b, preferred_element_type=F32)


def _dot_nt(a, b):
    return lax.dot_general(a, b, (((1,), (1,)), ((), ())), preferred_element_type=F32)


def _layer_norm(v, g, b):
    mu = jnp.mean(v, axis=-1, keepdims=True)
    d = v - mu
    var = jnp.mean(d * d, axis=-1, keepdims=True)
    return d * lax.rsqrt(var + LN_EPS) * g + b


def _params(n_axes):
    return pltpu.CompilerParams(dimension_semantics=("arbitrary",) * n_axes,
                                vmem_limit_bytes=VMEM_LIMIT)


def _resident(stacked, layer, block=None, index=None):
    block = tuple(stacked.shape[1:]) if block is None else block
    index = (0,) * len(block) if index is None else index
    return pl.BlockSpec((None,) + block, lambda *_: (layer,) + index,
                        pipeline_mode=pl.Buffered(1))


def _mod_spec(layer, row_map):
    return pl.BlockSpec((None, None, 6, D_MODEL), lambda *g: (layer, row_map(*g), 0, 0))


def _mod_kernel(cc_ref, w_ref, b_ref, o_ref):
    s = jax.nn.silu(cc_ref[...]).astype(BF16)
    o_ref[...] = _dot(s, w_ref[...].astype(BF16)) + b_ref[...]


def _modulation(cc, w_ada, b_ada):
    tn = 1536
    n6 = 6 * D_MODEL
    out = pl.pallas_call(
        _mod_kernel,
        grid=(DEPTH, n6 // tn),
        in_specs=[pl.BlockSpec((MOD_ROWS, D_MODEL), lambda l, n: (0, 0)),
                  pl.BlockSpec((None, D_MODEL, tn), lambda l, n: (l, 0, n)),
                  pl.BlockSpec((None, 1, tn), lambda l, n: (l, 0, n))],
        out_specs=pl.BlockSpec((None, MOD_ROWS, tn), lambda l, n: (l, 0, n)),
        out_shape=jax.ShapeDtypeStruct((DEPTH, MOD_ROWS, n6), F32),
        compiler_params=_params(2),
        name="adaln_modulation",
    )(cc, w_ada, b_ada.reshape(DEPTH, 1, n6))
    return out.reshape(DEPTH, MOD_ROWS, 6, D_MODEL)


def _rope_store(z, cos, sin, first_half, out_ref, scale):
    for j in range(NA_WIDTH // LANES):
        zj = z[:, j * LANES:(j + 1) * LANES]
        partner = jnp.where(first_half, pltpu.roll(zj, LANES - 16, 1), pltpu.roll(zj, 16, 1))
        r = zj * cos + partner * sin
        if scale != 1.0:
            r = r * scale
        out_ref[:, j * LANES:(j + 1) * LANES] = r.astype(out_ref.dtype)


def _inproj_kernel(*refs, mode):
    if mode == "latent":
        (x_ref, mod_ref, w_ref, lng_ref, lnb_ref, cos_ref, sin_ref,
         qr_ref, qp_ref, kr_ref, v_ref, gu_ref, svn_ref, sga_ref, sgb_ref) = refs
    elif mode == "ctx":
        (x_ref, mod_ref, w_ref, lng_ref, lnb_ref,
         qp_ref, kr_ref, v_ref, gu_ref, svn_ref, sga_ref, sgb_ref) = refs
    else:
        x_ref, mod_ref, wk_ref, wv_ref, kr_ref, v_ref = refs

    scale = HEAD_DIM ** -0.5
    h = (x_ref[...] * (1.0 + mod_ref[1:2, :]) + mod_ref[0:1, :]).astype(BF16)

    if mode == "ctx_kv":
        kr_ref[...] = _dot(h, wk_ref[...]).astype(BF16)
        v_ref[...] = _dot(h, wv_ref[...]).astype(BF16)
        return

    def proj(lo, width):
        return _dot(h, w_ref[:, lo:lo + width])

    zq = proj(0, NA_WIDTH)
    zk = proj(NA_WIDTH, NA_WIDTH)
    qp_ref[...] = (zq * scale).astype(BF16)
    if mode == "latent":
        cos = cos_ref[...]
        sin = sin_ref[...]
        lane = lax.broadcasted_iota(jnp.int32, cos.shape, 1)
        first_half = (lane % 32) < 16
        _rope_store(zq, cos, sin, first_half, qr_ref, scale)
        _rope_store(zk, cos, sin, first_half, kr_ref, 1.0)
    else:
        kr_ref[...] = zk.astype(BF16)
    v_ref[...] = proj(2 * NA_WIDTH, NA_WIDTH).astype(BF16)
    gu_ref[...] = jax.nn.gelu(proj(3 * NA_WIDTH, SG_WIDTH)).astype(BF16)
    sv = jax.nn.gelu(proj(3 * NA_WIDTH + SG_WIDTH, SG_WIDTH))
    svn_ref[...] = _layer_norm(sv, lng_ref[...], lnb_ref[...]).astype(BF16)
    lo = 3 * NA_WIDTH + 2 * SG_WIDTH
    sigmoid = lambda z: 0.5 * jnp.tanh(0.5 * z) + 0.5
    sga_ref[...] = sigmoid(proj(lo, D_MODEL)).astype(BF16)
    sgb_ref[...] = sigmoid(proj(lo + D_MODEL, D_MODEL)).astype(BF16)


def _in_projection(x2d, mods, w, lng, lnb, rope, *, layer, mode, tm, seq):
    t = x2d.shape[0]
    tiles_per_seq = seq // tm
    if mode == "latent":
        mod_spec = _mod_spec(layer, lambda i: i // tiles_per_seq)
    else:
        mod_spec = _mod_spec(layer, lambda i: CTX_MOD_ROW)
    tok = lambda width: pl.BlockSpec((tm, width), lambda i: (i, 0))
    if mode == "ctx_kv":
        in_specs = [tok(D_MODEL), mod_spec,
                    _resident(w, layer, (D_MODEL, NA_WIDTH), (0, 1)),
                    _resident(w, layer, (D_MODEL, NA_WIDTH), (0, 2))]
        args = [x2d, mods, w, w]
    else:
        in_specs = [tok(D_MODEL), mod_spec, _resident(w, layer),
                    _resident(lng, layer), _resident(lnb, layer)]
        args = [x2d, mods, w, lng, lnb]
    if mode == "latent":
        pos = pl.BlockSpec((tm, LANES), lambda i: (i % tiles_per_seq, 0))
        in_specs += [pos, pos]
        args += list(rope)
    widths = {"latent": [NA_WIDTH] * 6 + [D_MODEL] * 2,
              "ctx": [NA_WIDTH] * 5 + [D_MODEL] * 2,
              "ctx_kv": [NA_WIDTH] * 2}[mode]
    return pl.pallas_call(
        functools.partial(_inproj_kernel, mode=mode),
        grid=(t // tm,),
        in_specs=in_specs,
        out_specs=[tok(wd) for wd in widths],
        out_shape=[jax.ShapeDtypeStruct((t, wd), BF16) for wd in widths],
        compiler_params=_params(1),
        name="in_projection_" + mode,
    )(*args)


def _mix_kernel(*refs, latent, n_sub):
    n_nbr = n_sub + 2
    if latent:
        x_ref, mod_ref, qr_ref, qp_ref = refs[:4]
        k_refs = refs[4:4 + n_nbr]
        v_refs = refs[4 + n_nbr:4 + 2 * n_nbr]
        (kc_ref, vc_ref, tab_ref, gu_ref, svn_ref, sga_ref, sgb_ref, ws_ref, bs_ref,
         wpa_ref, wpb_ref, wo_ref, lng_ref, lnb_ref, o_ref, oa_ref, ob_ref) = refs[4 + 2 * n_nbr:]
    else:
        (x_ref, mod_ref, qp_ref, kc_ref, vc_ref, gu_ref, svn_ref, sga_ref, sgb_ref,
         ws_ref, bs_ref, wpa_ref, wpb_ref, wo_ref, lng_ref, lnb_ref,
         o_ref, oa_ref, ob_ref) = refs

    lane = lax.broadcasted_iota(jnp.int32, (Q_BLOCK, LANES), 1)
    low_half = lane < HEAD_DIM
    first_step = pl.program_id(1) == 0
    last_step = pl.program_id(1) == pl.num_programs(1) - 1

    for sub in range(n_sub):
        rows = slice(sub * Q_BLOCK, (sub + 1) * Q_BLOCK)
        tabs = [TAB_LOW, TAB_MID, TAB_HIGH]
        if latent and sub == 0:
            tabs[0] = jnp.where(first_step, TAB_MASKED, tabs[0])
            tabs[2] = jnp.where(first_step, TAB_HIGH_FULL, tabs[2])
        if latent and sub == n_sub - 1:
            tabs[0] = jnp.where(last_step, TAB_LOW_FULL, tabs[0])
            tabs[2] = jnp.where(last_step, TAB_MASKED, tabs[2])
        for p in range(HEAD_PAIRS):
            cols = slice(p * LANES, (p + 1) * LANES)

            def per_head(q):
                return jnp.concatenate([jnp.where(low_half, q, 0), jnp.where(low_half, 0, q)], axis=0)

            s_parts = [_dot_nt(per_head(qp_ref[rows, cols]), kc_ref[:, cols])]
            if latent:
                q_lat = per_head(qr_ref[rows, cols])
                for m in range(3):
                    s = _dot_nt(q_lat, k_refs[sub + m][:, cols])
                    tab = tab_ref[tabs[m], 2 * p:2 * p + 2].reshape(2 * Q_BLOCK, Q_BLOCK)
                    s_parts.append(s + tab)
            mx = functools.reduce(jnp.maximum, [s.max(axis=-1, keepdims=True) for s in s_parts])
            p_parts = [jnp.exp(s - mx) for s in s_parts]
            denom = functools.reduce(jnp.add, [pp.sum(axis=-1, keepdims=True) for pp in p_parts])
            acc = _dot(p_parts[0].astype(BF16), vc_ref[:, cols])
            for m in range(len(p_parts) - 1):
                acc = acc + _dot(p_parts[m + 1].astype(BF16), v_refs[sub + m][:, cols])
            o = acc / denom
            oa_ref[rows, cols] = jnp.where(low_half, o[:Q_BLOCK], o[Q_BLOCK:]).astype(BF16)

    half = lax.broadcasted_iota(jnp.int32, (CHUNK, LANES), 1) < (SG_WIDTH // SG_GROUPS)
    for c2 in range(n_sub * Q_BLOCK // (2 * CHUNK)):
        chunk_rows = [slice((2 * c2 + i) * CHUNK, (2 * c2 + i + 1) * CHUNK) for i in range(2)]
        for p in range(SG_GROUPS // 2):
            cols = slice(p * LANES, (p + 1) * LANES)
            vl = jnp.concatenate([svn_ref[r, cols] for r in chunk_rows], axis=1)
            both = _dot(ws_ref[p], vl)
            for i, r in enumerate(chunk_rows):
                lanes = slice(i * LANES, (i + 1) * LANES)
                mixed = jnp.where(half, both[:CHUNK, lanes], both[CHUNK:, lanes]) + bs_ref[p]
                ob_ref[r, cols] = (gu_ref[r, cols].astype(F32) * mixed).astype(BF16)

    for sub in range(n_sub):
        rows = slice(sub * Q_BLOCK, (sub + 1) * Q_BLOCK)
        ya = _dot(oa_ref[rows, :], wpa_ref[...])
        yb = _dot(ob_ref[rows, :], wpb_ref[...])
        y = sga_ref[rows, :].astype(F32) * ya + sgb_ref[rows, :].astype(F32) * yb
        out = _dot(y.astype(BF16), wo_ref[...])
        r = ALPHA * x_ref[rows, :] + mod_ref[2:3, :] * out
        o_ref[rows, :] = _layer_norm(r, lng_ref[...], lnb_ref[...])


def _mix(x2d, mods, acts, ctx_kv, tabs, wts, *, layer, latent, n_sub):
    t = x2d.shape[0]
    seq = SEQ if latent else CTX_LEN
    tq = n_sub * Q_BLOCK
    steps = seq // tq
    blocks_per_seq = seq // Q_BLOCK
    tok = lambda width: pl.BlockSpec((tq, width), lambda b, j: (b * steps + j, 0))
    ctx_spec = pl.BlockSpec((CTX_LEN, NA_WIDTH), lambda b, j: (b, 0))
    w_specs = [_resident(a, layer) for a in wts]
    if latent:
        qr, qp, kr, v, gu, svn, sga, sgb = acts
        kc, vc = ctx_kv

        def nbr(m):
            return pl.BlockSpec(
                (Q_BLOCK, NA_WIDTH),
                lambda b, j: (b * blocks_per_seq
                              + jnp.clip(n_sub * j - 1 + m, 0, blocks_per_seq - 1), 0))

        nbrs = [nbr(m) for m in range(n_sub + 2)]
        in_specs = ([tok(D_MODEL), _mod_spec(layer, lambda b, j: b),
                     tok(NA_WIDTH), tok(NA_WIDTH)] + nbrs + nbrs
                    + [ctx_spec, ctx_spec, _resident(tabs, layer)]
                    + [tok(NA_WIDTH), tok(NA_WIDTH), tok(D_MODEL), tok(D_MODEL)] + w_specs)
        args = ([x2d, mods, qr, qp] + [kr] * (n_sub + 2) + [v] * (n_sub + 2)
                + [kc, vc, tabs, gu, svn, sga, sgb] + list(wts))
    else:
        qp, kc, vc, gu, svn, sga, sgb = acts
        in_specs = ([tok(D_MODEL), _mod_spec(layer, lambda b, j: CTX_MOD_ROW),
                     tok(NA_WIDTH), ctx_spec, ctx_spec]
                    + [tok(NA_WIDTH), tok(NA_WIDTH), tok(D_MODEL), tok(D_MODEL)] + w_specs)
        args = [x2d, mods, qp, kc, vc, gu, svn, sga, sgb] + list(wts)
    return pl.pallas_call(
        functools.partial(_mix_kernel, latent=latent, n_sub=n_sub),
        grid=(BATCH, steps),
        in_specs=in_specs,
        out_specs=tok(D_MODEL),
        out_shape=jax.ShapeDtypeStruct((t, D_MODEL), F32),
        scratch_shapes=[pltpu.VMEM((tq, NA_WIDTH), BF16),
                        pltpu.VMEM((tq, SG_WIDTH), BF16)],
        compiler_params=_params(2),
        name="mix_latent" if latent else "mix_ctx",
    )(*args)


def _ffn_kernel(x_ref, prev_ref, next_ref, mod_ref, wu_ref, cw_ref, cb_ref, wd_ref,
                lng_ref, lnb_ref, o_ref, h_ref, act_ref, f_ref, *z_refs, tm, seq):
    i = pl.program_id(0)
    if seq >= tm:
        tiles_per_seq = seq // tm
        has_prev = (i % tiles_per_seq != 0).astype(F32)
        has_next = (i % tiles_per_seq != tiles_per_seq - 1).astype(F32)
        inside = None
    else:
        has_prev = has_next = 0.0
        pos = lax.broadcasted_iota(jnp.int32, (tm, LANES), 0) % seq
        inside = ((pos != 0).astype(F32), (pos != seq - 1).astype(F32))
    shift = mod_ref[3:4, :]
    scale1 = 1.0 + mod_ref[4:5, :]
    rows = tm + 2 * HALO
    h_ref[0:tm, :] = (x_ref[...] * scale1 + shift).astype(BF16)
    h_ref[tm:rows, :] = jnp.concatenate(
        [(prev_ref[...] * scale1 + shift) * has_prev,
         (next_ref[...] * scale1 + shift) * has_next], axis=0).astype(BF16)

    def conv(lo, z_ref):
        z = _dot(h_ref[...], wu_ref[:, lo:lo + FF_CHUNK])
        outs = []
        for t in range(FF_CHUNK // LANES):
            cols = slice(lo + t * LANES, lo + (t + 1) * LANES)
            zt = z[:, t * LANES:(t + 1) * LANES]
            z_ref[t, 0:HALO, :] = zt[tm:tm + HALO]
            z_ref[t, HALO:HALO + tm, :] = zt[0:tm]
            z_ref[t, HALO + tm:rows, :] = zt[tm + HALO:rows]
            before = z_ref[t, HALO - 1:HALO - 1 + tm, :]
            after = z_ref[t, HALO + 1:HALO + 1 + tm, :]
            if inside is not None:
                before, after = before * inside[0], after * inside[1]
            outs.append(cw_ref[0:1, cols] * before + cw_ref[1:2, cols] * zt[0:tm]
                        + cw_ref[2:3, cols] * after + cb_ref[0:1, cols])
        return jnp.concatenate(outs, axis=1)

    def down(lo, width):
        return _dot(act_ref[:, lo:lo + width], wd_ref[lo:lo + width, :])

    assert N_FF_CHUNKS % 2 == 1
    for c in range(N_FF_CHUNKS):
        lo = c * FF_CHUNK
        zg_ref, za_ref = z_refs[2 * (c % 2)], z_refs[2 * (c % 2) + 1]
        g = 0.5 * conv(D_FF + lo, zg_ref)
        act_ref[:, lo:lo + FF_CHUNK] = ((g + g * jnp.tanh(g)) * conv(lo, za_ref)).astype(BF16)
        if c == 1:
            f_ref[...] = down(0, 2 * FF_CHUNK)
        elif c % 2 == 1:
            f_ref[...] += down(lo - FF_CHUNK, 2 * FF_CHUNK)
    f = f_ref[...] + down(D_FF - FF_CHUNK, FF_CHUNK)
    r = ALPHA * x_ref[...] + mod_ref[5:6, :] * f
    o_ref[...] = _layer_norm(r, lng_ref[...], lnb_ref[...])


def _conv_ffn(x2d, mods, wts, *, layer, latent, tm, seq):
    t = x2d.shape[0]
    assert seq % tm == 0 or (tm % seq == 0 and not latent)
    n_halo_blocks = t // HALO
    per_tile = tm // HALO
    if latent:
        mod_spec = _mod_spec(layer, lambda i: i // (seq // tm))
    else:
        mod_spec = _mod_spec(layer, lambda i: CTX_MOD_ROW)
    tok = pl.BlockSpec((tm, D_MODEL), lambda i: (i, 0))
    prev = pl.BlockSpec((HALO, D_MODEL), lambda i: (jnp.maximum(i * per_tile - 1, 0), 0))
    nxt = pl.BlockSpec((HALO, D_MODEL),
                       lambda i: (jnp.minimum((i + 1) * per_tile, n_halo_blocks - 1), 0))
    return pl.pallas_call(
        functools.partial(_ffn_kernel, tm=tm, seq=seq),
        grid=(t // tm,),
        in_specs=[tok, prev, nxt, mod_spec] + [_resident(a, layer) for a in wts],
        out_specs=tok,
        out_shape=jax.ShapeDtypeStruct((t, D_MODEL), F32),
        scratch_shapes=[pltpu.VMEM((tm + 2 * HALO, D_MODEL), BF16),
                        pltpu.VMEM((tm, D_FF), BF16),
                        pltpu.VMEM((tm, D_MODEL), F32)]
                       + [pltpu.VMEM((FF_CHUNK // LANES, tm + 2 * HALO, LANES), F32)] * 4,
        compiler_params=_params(1),
        name="conv_ffn_latent" if latent else "conv_ffn_ctx",
    )(x2d, x2d, x2d, mods, *wts)


def _rope_tables():
    half = HEAD_DIM // 2
    nf = half // 2
    inv = ROPE_BASE ** (-jnp.arange(nf, dtype=F32) / nf)
    t = jnp.arange(SEQ)

    def part(pos):
        ang = pos.astype(F32)[:, None] * inv[None, :]
        cos = jnp.cos(ang)
        sin = jnp.sin(ang)
        return jnp.concatenate([cos, cos], axis=-1), jnp.concatenate([-sin, sin], axis=-1)

    cr, sr = part(t // GRID_W)
    cc, sc = part(t % GRID_W)
    cos = jnp.concatenate([cr, cc], axis=-1)
    sin = jnp.concatenate([sr, sc], axis=-1)
    reps = LANES // HEAD_DIM
    return jnp.tile(cos, (1, reps)), jnp.tile(sin, (1, reps))


def _row_ok(kind, qi, ki):
    if kind == TAB_MASKED:
        return False
    if kind in (TAB_LOW_FULL, TAB_MID, TAB_HIGH_FULL):
        return True
    return qi <= ki < qi + WIN_H


BIAS_HI = 32


def _bias_kernel(rp_ref, o_ref, pt_ref):
    n_pairs = 2 * WIN_H - 2
    qc = lax.broadcasted_iota(jnp.int32, (GRID_W, LANES), 0)
    lane = lax.broadcasted_iota(jnp.int32, (GRID_W, LANES), 1)
    upper = lane >= GRID_W
    kc = jnp.where(upper, lane - GRID_W, lane)
    win_start = jnp.clip(qc - WIN_W // 2, 0, GRID_W - WIN_W)
    col_ok = (kc >= win_start) & (kc < win_start + WIN_W)
    idx = jnp.clip(kc - qc + WIN_W - 1, 0, 2 * WIN_W - 2) + jnp.where(upper, BIAS_HI, 0)
    masked = jnp.full((GRID_W, LANES), NEG_INF, F32)
    for dy in range(n_pairs):
        src = jnp.broadcast_to(rp_ref[dy:dy + 1, :], (GRID_W, LANES))
        pt_ref[dy] = jnp.where(col_ok, jnp.take_along_axis(src, idx, axis=1), NEG_INF)
    for kind in range(N_TAB_KINDS):
        for qi in range(Q_ROWS):
            for kp in range(Q_ROWS // 2):
                ki = Q_ROWS * TAB_KEY_BLOCK[kind] + 2 * kp
                ok = (_row_ok(kind, qi, ki), _row_ok(kind, qi, ki + 1))
                tile = pt_ref[ki - qi + WIN_H // 2 - 1] if any(ok) else masked
                if ok == (True, False):
                    tile = jnp.where(upper, NEG_INF, tile)
                elif ok == (False, True):
                    tile = jnp.where(upper, tile, NEG_INF)
                o_ref[kind, qi * GRID_W:(qi + 1) * GRID_W, kp * LANES:(kp + 1) * LANES] = tile


def _bias_tables(rpb):
    n_dx = 2 * WIN_W - 1
    assert n_dx <= BIAS_HI and BIAS_HI + n_dx <= LANES
    pad = lambda a, n: jnp.pad(a, ((0, 0), (0, 0), (0, 2), (0, n - a.shape[-1])))
    rp = jnp.concatenate([pad(rpb[:, :, :-1], BIAS_HI), pad(rpb[:, :, 1:], LANES - BIAS_HI)], axis=-1)
    rows = rp.shape[2]
    return pl.pallas_call(
        _bias_kernel,
        grid=(DEPTH, NA_HEADS),
        in_specs=[pl.BlockSpec((None, None, rows, LANES), lambda l, h: (l, h, 0, 0))],
        out_specs=pl.BlockSpec((None, N_TAB_KINDS, None, Q_BLOCK, Q_BLOCK),
                               lambda l, h: (l, 0, h, 0, 0)),
        out_shape=jax.ShapeDtypeStruct((DEPTH, N_TAB_KINDS, NA_HEADS, Q_BLOCK, Q_BLOCK), F32),
        scratch_shapes=[pltpu.VMEM((2 * WIN_H - 2, GRID_W, LANES), F32)],
        compiler_params=_params(2),
        name="bias_tables",
    )(rp)


def _stacked_weights(w_in, sg_ln_g, sg_ln_b, w_s, b_s, w_pa, w_pb, w_o, ln1_g, ln1_b,
                     w_up, conv_w, conv_b, w_down, ln2_g, ln2_b):
    row = lambda a: a.reshape(DEPTH, 1, -1)
    bf16 = lambda a: a.astype(BF16)
    bs = jnp.repeat(b_s.reshape(DEPTH, SG_GROUPS // 2, 2, CHUNK).transpose(0, 1, 3, 2),
                    SG_WIDTH // SG_GROUPS, axis=-1)
    ws = bf16(w_s).reshape(DEPTH, SG_GROUPS // 2, 2 * CHUNK, CHUNK)
    inproj = (bf16(w_in), row(sg_ln_g), row(sg_ln_b))
    mix = (ws, bs, bf16(w_pa), bf16(w_pb), bf16(w_o), row(ln1_g), row(ln1_b))
    ffn = (bf16(w_up), conv_w, row(conv_b), bf16(w_down), row(ln2_g), row(ln2_b))
    return inproj, mix, ffn


def kernel(x, c, ctx, c_ctx, w_ada, b_ada, w_in, rpb, sg_ln_g, sg_ln_b, w_s, b_s, w_pa, w_pb, w_o,
           ln1_g, ln1_b, w_up, conv_w, conv_b, w_down, ln2_g, ln2_b):
    assert x.shape == (BATCH, SEQ, D_MODEL) and ctx.shape == (BATCH, CTX_LEN, D_MODEL)
    cc = jnp.concatenate([c, c_ctx[None, :], jnp.zeros((MOD_ROWS - BATCH - 1, D_MODEL), F32)], axis=0)
    mods = _modulation(cc, w_ada, b_ada)
    rope = _rope_tables()
    tabs = _bias_tables(rpb)
    (w_in_b, lng, lnb), mix_w, ffn_w = _stacked_weights(
        w_in, sg_ln_g, sg_ln_b, w_s, b_s, w_pa, w_pb, w_o, ln1_g, ln1_b,
        w_up, conv_w, conv_b, w_down, ln2_g, ln2_b)
    xl = x.reshape(BATCH * SEQ, D_MODEL)
    xc = ctx.reshape(BATCH * CTX_LEN, D_MODEL)
    ctx_tiles = dict(tm=2 * CTX_LEN, seq=2 * CTX_LEN)
    for i in range(DEPTH):
        if i < DEPTH - 1:
            c_acts = _in_projection(xc, mods, w_in_b, lng, lnb, None, layer=i, mode="ctx",
                                    **ctx_tiles)
            ctx_kv = (c_acts[1], c_acts[2])
        else:
            ctx_kv = _in_projection(xc, mods, w_in_b, None, None, None, layer=i, mode="ctx_kv",
                                    **ctx_tiles)
        acts = _in_projection(xl, mods, w_in_b, lng, lnb, rope, layer=i, mode="latent", tm=1024,
                              seq=SEQ)
        xl = _mix(xl, mods, acts, ctx_kv, tabs, mix_w, layer=i, latent=True, n_sub=2)
        xl = _conv_ffn(xl, mods, ffn_w, layer=i, latent=True, tm=512, seq=SEQ)
        if i < DEPTH - 1:
            xc = _mix(xc, mods, c_acts, None, None, mix_w, layer=i, latent=False, n_sub=1)
            xc = _conv_ffn(xc, mods, ffn_w, layer=i, latent=False, tm=2 * CTX_LEN, seq=CTX_LEN)
    return xl.reshape(BATCH, SEQ, D_MODEL)
```

```python
import functools

import jax
import jax.numpy as jnp
from jax import lax
from jax.experimental import pallas as pl
from jax.experimental.pallas import tpu as pltpu

D_MODEL = 1024
BATCH = 4
SEQ = 4096
DEPTH = 2
GRID_W = 64
CTX_LEN = 256
NA_HEADS = 8
HEAD_DIM = 64
NA_WIDTH = NA_HEADS * HEAD_DIM
WIN_H = 8
WIN_W = 16
ROPE_BASE = 10000.0
SG_GROUPS = 8
SG_WIDTH = 512
CHUNK = 128
D_FF = 2816
ALPHA = (2 * DEPTH) ** 0.25
LN_EPS = 1e-5
NEG_INF = -1e30

F32 = jnp.float32
BF16 = jnp.bfloat16

LANES = 128
HEAD_PAIRS = NA_HEADS // 2
MOD_ROWS = 8
CTX_MOD_ROW = BATCH
Q_ROWS = 4
Q_BLOCK = Q_ROWS * GRID_W
K_ROWS = 3 * Q_ROWS
TAB_LOW, TAB_LOW_FULL, TAB_MID, TAB_HIGH, TAB_HIGH_FULL, TAB_MASKED = range(6)
TAB_KEY_BLOCK = (0, 0, 1, 2, 2, 0)
N_TAB_KINDS = len(TAB_KEY_BLOCK)
FF_CHUNK = 256
N_FF_CHUNKS = D_FF // FF_CHUNK
HALO = 8
VMEM_LIMIT = 56 * 1024 * 1024


def _dot(a, b):
    return jnp.dot(a, b, preferred_element_type=F32)


def _dot_nt(a, b):
    return lax.dot_general(a, b, (((1,), (1,)), ((), ())), preferred_element_type=F32)


def _layer_norm(v, g, b):
    mu = jnp.mean(v, axis=-1, keepdims=True)
    d = v - mu
    var = jnp.mean(d * d, axis=-1, keepdims=True)
    return d * lax.rsqrt(var + LN_EPS) * g + b


def _params(n_axes):
    return pltpu.CompilerParams(dimension_semantics=("arbitrary",) * n_axes,
                                vmem_limit_bytes=VMEM_LIMIT)


def _resident(stacked, layer, block=None, index=None):
    block = tuple(stacked.shape[1:]) if block is None else block
    index = (0,) * len(block) if index is None else index
    return pl.BlockSpec((None,) + block, lambda *_: (layer,) + index,
                        pipeline_mode=pl.Buffered(1))


def _mod_spec(layer, row_map):
    return pl.BlockSpec((None, None, 6, D_MODEL), lambda *g: (layer, row_map(*g), 0, 0))


def _mod_kernel(cc_ref, w_ref, b_ref, o_ref):
    s = jax.nn.silu(cc_ref[...]).astype(BF16)
    o_ref[...] = _dot(s, w_ref[...].astype(BF16)) + b_ref[...]


def _modulation(cc, w_ada, b_ada):
    tn = 1536
    n6 = 6 * D_MODEL
    out = pl.pallas_call(
        _mod_kernel,
        grid=(DEPTH, n6 // tn),
        in_specs=[pl.BlockSpec((MOD_ROWS, D_MODEL), lambda l, n: (0, 0)),
                  pl.BlockSpec((None, D_MODEL, tn), lambda l, n: (l, 0, n)),
                  pl.BlockSpec((None, 1, tn), lambda l, n: (l, 0, n))],
        out_specs=pl.BlockSpec((None, MOD_ROWS, tn), lambda l, n: (l, 0, n)),
        out_shape=jax.ShapeDtypeStruct((DEPTH, MOD_ROWS, n6), F32),
        compiler_params=_params(2),
        name="adaln_modulation",
    )(cc, w_ada, b_ada.reshape(DEPTH, 1, n6))
    return out.reshape(DEPTH, MOD_ROWS, 6, D_MODEL)


def _rope_store(z, cos, sin, first_half, out_ref, scale):
    for j in range(NA_WIDTH // LANES):
        zj = z[:, j * LANES:(j + 1) * LANES]
        partner = jnp.where(first_half, pltpu.roll(zj, LANES - 16, 1), pltpu.roll(zj, 16, 1))
        r = zj * cos + partner * sin
        if scale != 1.0:
            r = r * scale
        out_ref[:, j * LANES:(j + 1) * LANES] = r.astype(out_ref.dtype)


def _inproj_kernel(*refs, mode):
    if mode == "latent":
        (x_ref, mod_ref, w_ref, lng_ref, lnb_ref, cos_ref, sin_ref,
         qr_ref, qp_ref, kr_ref, v_ref, gu_ref, svn_ref, sga_ref, sgb_ref) = refs
    elif mode == "ctx":
        (x_ref, mod_ref, w_ref, lng_ref, lnb_ref,
         qp_ref, kr_ref, v_ref, gu_ref, svn_ref, sga_ref, sgb_ref) = refs
    else:
        x_ref, mod_ref, wk_ref, wv_ref, kr_ref, v_ref = refs

    scale = HEAD_DIM ** -0.5
    h = (x_ref[...] * (1.0 + mod_ref[1:2, :]) + mod_ref[0:1, :]).astype(BF16)

    if mode == "ctx_kv":
        kr_ref[...] = _dot(h, wk_ref[...]).astype(BF16)
        v_ref[...] = _dot(h, wv_ref[...]).astype(BF16)
        return

    def proj(lo, width):
        return _dot(h, w_ref[:, lo:lo + width])

    zq = proj(0, NA_WIDTH)
    zk = proj(NA_WIDTH, NA_WIDTH)
    qp_ref[...] = (zq * scale).astype(BF16)
    if mode == "latent":
        cos = cos_ref[...]
        sin = sin_ref[...]
        lane = lax.broadcasted_iota(jnp.int32, cos.shape, 1)
        first_half = (lane % 32) < 16
        _rope_store(zq, cos, sin, first_half, qr_ref, scale)
        _rope_store(zk, cos, sin, first_half, kr_ref, 1.0)
    else:
        kr_ref[...] = zk.astype(BF16)
    v_ref[...] = proj(2 * NA_WIDTH, NA_WIDTH).astype(BF16)
    gu_ref[...] = jax.nn.gelu(proj(3 * NA_WIDTH, SG_WIDTH)).astype(BF16)
    sv = jax.nn.gelu(proj(3 * NA_WIDTH + SG_WIDTH, SG_WIDTH))
    svn_ref[...] = _layer_norm(sv, lng_ref[...], lnb_ref[...]).astype(BF16)
    lo = 3 * NA_WIDTH + 2 * SG_WIDTH
    sigmoid = lambda z: 0.5 * jnp.tanh(0.5 * z) + 0.5
    sga_ref[...] = sigmoid(proj(lo, D_MODEL)).astype(BF16)
    sgb_ref[...] = sigmoid(proj(lo + D_MODEL, D_MODEL)).astype(BF16)


def _in_projection(x2d, mods, w, lng, lnb, rope, *, layer, mode, tm, seq):
    t = x2d.shape[0]
    tiles_per_seq = seq // tm
    if mode == "latent":
        mod_spec = _mod_spec(layer, lambda i: i // tiles_per_seq)
    else:
        mod_spec = _mod_spec(layer, lambda i: CTX_MOD_ROW)
    tok = lambda width: pl.BlockSpec((tm, width), lambda i: (i, 0))
    if mode == "ctx_kv":
        in_specs = [tok(D_MODEL), mod_spec,
                    _resident(w, layer, (D_MODEL, NA_WIDTH), (0, 1)),
                    _resident(w, layer, (D_MODEL, NA_WIDTH), (0, 2))]
        args = [x2d, mods, w, w]
    else:
        in_specs = [tok(D_MODEL), mod_spec, _resident(w, layer),
                    _resident(lng, layer), _resident(lnb, layer)]
        args = [x2d, mods, w, lng, lnb]
    if mode == "latent":
        pos = pl.BlockSpec((tm, LANES), lambda i: (i % tiles_per_seq, 0))
        in_specs += [pos, pos]
        args += list(rope)
    widths = {"latent": [NA_WIDTH] * 6 + [D_MODEL] * 2,
              "ctx": [NA_WIDTH] * 5 + [D_MODEL] * 2,
              "ctx_kv": [NA_WIDTH] * 2}[mode]
    return pl.pallas_call(
        functools.partial(_inproj_kernel, mode=mode),
        grid=(t // tm,),
        in_specs=in_specs,
        out_specs=[tok(wd) for wd in widths],
        out_shape=[jax.ShapeDtypeStruct((t, wd), BF16) for wd in widths],
        compiler_params=_params(1),
        name="in_projection_" + mode,
    )(*args)


def _mix_kernel(*refs, latent, n_sub):
    n_nbr = n_sub + 2
    if latent:
        x_ref, mod_ref, qr_ref, qp_ref = refs[:4]
        k_refs = refs[4:4 + n_nbr]
        v_refs = refs[4 + n_nbr:4 + 2 * n_nbr]
        (kc_ref, vc_ref, tab_ref, gu_ref, svn_ref, sga_ref, sgb_ref, ws_ref, bs_ref,
         wpa_ref, wpb_ref, wo_ref, lng_ref, lnb_ref, o_ref, oa_ref, ob_ref) = refs[4 + 2 * n_nbr:]
    else:
        (x_ref, mod_ref, qp_ref, kc_ref, vc_ref, gu_ref, svn_ref, sga_ref, sgb_ref,
         ws_ref, bs_ref, wpa_ref, wpb_ref, wo_ref, lng_ref, lnb_ref,
         o_ref, oa_ref, ob_ref) = refs

    lane = lax.broadcasted_iota(jnp.int32, (Q_BLOCK, LANES), 1)
    low_half = lane < HEAD_DIM
    first_step = pl.program_id(1) == 0
    last_step = pl.program_id(1) == pl.num_programs(1) - 1

    for sub in range(n_sub):
        rows = slice(sub * Q_BLOCK, (sub + 1) * Q_BLOCK)
        tabs = [TAB_LOW, TAB_MID, TAB_HIGH]
        if latent and sub == 0:
            tabs[0] = jnp.where(first_step, TAB_MASKED, tabs[0])
            tabs[2] = jnp.where(first_step, TAB_HIGH_FULL, tabs[2])
        if latent and sub == n_sub - 1:
            tabs[0] = jnp.where(last_step, TAB_LOW_FULL, tabs[0])
            tabs[2] = jnp.where(last_step, TAB_MASKED, tabs[2])
        for p in range(HEAD_PAIRS):
            cols = slice(p * LANES, (p + 1) * LANES)

            def per_head(q):
                return jnp.concatenate([jnp.where(low_half, q, 0), jnp.where(low_half, 0, q)], axis=0)

            s_parts = [_dot_nt(per_head(qp_ref[rows, cols]), kc_ref[:, cols])]
            if latent:
                q_lat = per_head(qr_ref[rows, cols])
                for m in range(3):
                    s = _dot_nt(q_lat, k_refs[sub + m][:, cols])
                    tab = tab_ref[tabs[m], 2 * p:2 * p + 2].reshape(2 * Q_BLOCK, Q_BLOCK)
                    s_parts.append(s + tab)
            mx = functools.reduce(jnp.maximum, [s.max(axis=-1, keepdims=True) for s in s_parts])
            p_parts = [jnp.exp(s - mx) for s in s_parts]
            denom = functools.reduce(jnp.add, [pp.sum(axis=-1, keepdims=True) for pp in p_parts])
            acc = _dot(p_parts[0].astype(BF16), vc_ref[:, cols])
            for m in range(len(p_parts) - 1):
                acc = acc + _dot(p_parts[m + 1].astype(BF16), v_refs[sub + m][:, cols])
            o = acc / denom
            oa_ref[rows, cols] = jnp.where(low_half, o[:Q_BLOCK], o[Q_BLOCK:]).astype(BF16)

    half = lax.broadcasted_iota(jnp.int32, (CHUNK, LANES), 1) < (SG_WIDTH // SG_GROUPS)
    for c2 in range(n_sub * Q_BLOCK // (2 * CHUNK)):
        chunk_rows = [slice((2 * c2 + i) * CHUNK, (2 * c2 + i + 1) * CHUNK) for i in range(2)]
        for p in range(SG_GROUPS // 2):
            cols = slice(p * LANES, (p + 1) * LANES)
            vl = jnp.concatenate([svn_ref[r, cols] for r in chunk_rows], axis=1)
            both = _dot(ws_ref[p], vl)
            for i, r in enumerate(chunk_rows):
                lanes = slice(i * LANES, (i + 1) * LANES)
                mixed = jnp.where(half, both[:CHUNK, lanes], both[CHUNK:, lanes]) + bs_ref[p]
                ob_ref[r, cols] = (gu_ref[r, cols].astype(F32) * mixed).astype(BF16)

    for sub in range(n_sub):
        rows = slice(sub * Q_BLOCK, (sub + 1) * Q_BLOCK)
        ya = _dot(oa_ref[rows, :], wpa_ref[...])
        yb = _dot(ob_ref[rows, :], wpb_ref[...])
        y = sga_ref[rows, :].astype(F32) * ya + sgb_ref[rows, :].astype(F32) * yb
        out = _dot(y.astype(BF16), wo_ref[...])
        r = ALPHA * x_ref[rows, :] + mod_ref[2:3, :] * out
        o_ref[rows, :] = _layer_norm(r, lng_ref[...], lnb_ref[...])


def _mix(x2d, mods, acts, ctx_kv, tabs, wts, *, layer, latent, n_sub):
    t = x2d.shape[0]
    seq = SEQ if latent else CTX_LEN
    tq = n_sub * Q_BLOCK
    steps = seq // tq
    blocks_per_seq = seq // Q_BLOCK
    tok = lambda width: pl.BlockSpec((tq, width), lambda b, j: (b * steps + j, 0))
    ctx_spec = pl.BlockSpec((CTX_LEN, NA_WIDTH), lambda b, j: (b, 0))
    w_specs = [_resident(a, layer) for a in wts]
    if latent:
        qr, qp, kr, v, gu, svn, sga, sgb = acts
        kc, vc = ctx_kv

        def nbr(m):
            return pl.BlockSpec(
                (Q_BLOCK, NA_WIDTH),
                lambda b, j: (b * blocks_per_seq
                              + jnp.clip(n_sub * j - 1 + m, 0, blocks_per_seq - 1), 0))

        nbrs = [nbr(m) for m in range(n_sub + 2)]
        in_specs = ([tok(D_MODEL), _mod_spec(layer, lambda b, j: b),
                     tok(NA_WIDTH), tok(NA_WIDTH)] + nbrs + nbrs
                    + [ctx_spec, ctx_spec, _resident(tabs, layer)]
                    + [tok(NA_WIDTH), tok(NA_WIDTH), tok(D_MODEL), tok(D_MODEL)] + w_specs)
        args = ([x2d, mods, qr, qp] + [kr] * (n_sub + 2) + [v] * (n_sub + 2)
                + [kc, vc, tabs, gu, svn, sga, sgb] + list(wts))
    else:
        qp, kc, vc, gu, svn, sga, sgb = acts
        in_specs = ([tok(D_MODEL), _mod_spec(layer, lambda b, j: CTX_MOD_ROW),
                     tok(NA_WIDTH), ctx_spec, ctx_spec]
                    + [tok(NA_WIDTH), tok(NA_WIDTH), tok(D_MODEL), tok(D_MODEL)] + w_specs)
        args = [x2d, mods, qp, kc, vc, gu, svn, sga, sgb] + list(wts)
    return pl.pallas_call(
        functools.partial(_mix_kernel, latent=latent, n_sub=n_sub),
        grid=(BATCH, steps),
        in_specs=in_specs,
        out_specs=tok(D_MODEL),
        out_shape=jax.ShapeDtypeStruct((t, D_MODEL), F32),
        scratch_shapes=[pltpu.VMEM((tq, NA_WIDTH), BF16),
                        pltpu.VMEM((tq, SG_WIDTH), BF16)],
        compiler_params=_params(2),
        name="mix_latent" if latent else "mix_ctx",
    )(*args)


def _ffn_kernel(x_ref, prev_ref, next_ref, mod_ref, wu_ref, cw_ref, cb_ref, wd_ref,
                lng_ref, lnb_ref, o_ref, h_ref, act_ref, f_ref, *z_refs, tm, seq):
    i = pl.program_id(0)
    if seq >= tm:
        tiles_per_seq = seq // tm
        has_prev = (i % tiles_per_seq != 0).astype(F32)
        has_next = (i % tiles_per_seq != tiles_per_seq - 1).astype(F32)
        inside = None
    else:
        has_prev = has_next = 0.0
        pos = lax.broadcasted_iota(jnp.int32, (tm, LANES), 0) % seq
        inside = ((pos != 0).astype(F32), (pos != seq - 1).astype(F32))
    shift = mod_ref[3:4, :]
    scale1 = 1.0 + mod_ref[4:5, :]
    rows = tm + 2 * HALO
    h_ref[0:tm, :] = (x_ref[...] * scale1 + shift).astype(BF16)
    h_ref[tm:rows, :] = jnp.concatenate(
        [(prev_ref[...] * scale1 + shift) * has_prev,
         (next_ref[...] * scale1 + shift) * has_next], axis=0).astype(BF16)

    def conv(lo, z_ref):
        z = _dot(h_ref[...], wu_ref[:, lo:lo + FF_CHUNK])
        outs = []
        for t in range(FF_CHUNK // LANES):
            cols = slice(lo + t * LANES, lo + (t + 1) * LANES)
            zt = z[:, t * LANES:(t + 1) * LANES]
            z_ref[t, 0:HALO, :] = zt[tm:tm + HALO]
            z_ref[t, HALO:HALO + tm, :] = zt[0:tm]
            z_ref[t, HALO + tm:rows, :] = zt[tm + HALO:rows]
            before = z_ref[t, HALO - 1:HALO - 1 + tm, :]
            after = z_ref[t, HALO + 1:HALO + 1 + tm, :]
            if inside is not None:
                before, after = before * inside[0], after * inside[1]
            outs.append(cw_ref[0:1, cols] * before + cw_ref[1:2, cols] * zt[0:tm]
                        + cw_ref[2:3, cols] * after + cb_ref[0:1, cols])
        return jnp.concatenate(outs, axis=1)

    def down(lo, width):
        return _dot(act_ref[:, lo:lo + width], wd_ref[lo:lo + width, :])

    assert N_FF_CHUNKS % 2 == 1
    for c in range(N_FF_CHUNKS):
        lo = c * FF_CHUNK
        zg_ref, za_ref = z_refs[2 * (c % 2)], z_refs[2 * (c % 2) + 1]
        act_ref[:, lo:lo + FF_CHUNK] = (jax.nn.silu(conv(D_FF + lo, zg_ref))
                                        * conv(lo, za_ref)).astype(BF16)
        if c == 1:
            f_ref[...] = down(0, 2 * FF_CHUNK)
        elif c % 2 == 1:
            f_ref[...] += down(lo - FF_CHUNK, 2 * FF_CHUNK)
    f = f_ref[...] + down(D_FF - FF_CHUNK, FF_CHUNK)
    r = ALPHA * x_ref[...] + mod_ref[5:6, :] * f
    o_ref[...] = _layer_norm(r, lng_ref[...], lnb_ref[...])


def _conv_ffn(x2d, mods, wts, *, layer, latent, tm, seq):
    t = x2d.shape[0]
    assert seq % tm == 0 or (tm % seq == 0 and not latent)
    n_halo_blocks = t // HALO
    per_tile = tm // HALO
    if latent:
        mod_spec = _mod_spec(layer, lambda i: i // (seq // tm))
    else:
        mod_spec = _mod_spec(layer, lambda i: CTX_MOD_ROW)
    tok = pl.BlockSpec((tm, D_MODEL), lambda i: (i, 0))
    prev = pl.BlockSpec((HALO, D_MODEL), lambda i: (jnp.maximum(i * per_tile - 1, 0), 0))
    nxt = pl.BlockSpec((HALO, D_MODEL),
                       lambda i: (jnp.minimum((i + 1) * per_tile, n_halo_blocks - 1), 0))
    return pl.pallas_call(
        functools.partial(_ffn_kernel, tm=tm, seq=seq),
        grid=(t // tm,),
        in_specs=[tok, prev, nxt, mod_spec] + [_resident(a, layer) for a in wts],
        out_specs=tok,
        out_shape=jax.ShapeDtypeStruct((t, D_MODEL), F32),
        scratch_shapes=[pltpu.VMEM((tm + 2 * HALO, D_MODEL), BF16),
                        pltpu.VMEM((tm, D_FF), BF16),
                        pltpu.VMEM((tm, D_MODEL), F32)]
                       + [pltpu.VMEM((FF_CHUNK // LANES, tm + 2 * HALO, LANES), F32)] * 4,
        compiler_params=_params(1),
        name="conv_ffn_latent" if latent else "conv_ffn_ctx",
    )(x2d, x2d, x2d, mods, *wts)


def _rope_tables():
    half = HEAD_DIM // 2
    nf = half // 2
    inv = ROPE_BASE ** (-jnp.arange(nf, dtype=F32) / nf)
    t = jnp.arange(SEQ)

    def part(pos):
        ang = pos.astype(F32)[:, None] * inv[None, :]
        cos = jnp.cos(ang)
        sin = jnp.sin(ang)
        return jnp.concatenate([cos, cos], axis=-1), jnp.concatenate([-sin, sin], axis=-1)

    cr, sr = part(t // GRID_W)
    cc, sc = part(t % GRID_W)
    cos = jnp.concatenate([cr, cc], axis=-1)
    sin = jnp.concatenate([sr, sc], axis=-1)
    reps = LANES // HEAD_DIM
    return jnp.tile(cos, (1, reps)), jnp.tile(sin, (1, reps))


def _row_ok(kind, qi, ki):
    if kind == TAB_MASKED:
        return False
    if kind in (TAB_LOW_FULL, TAB_MID, TAB_HIGH_FULL):
        return True
    return qi <= ki < qi + WIN_H


BIAS_HI = 32


def _bias_kernel(rp_ref, o_ref, pt_ref):
    n_pairs = 2 * WIN_H - 2
    qc = lax.broadcasted_iota(jnp.int32, (GRID_W, LANES), 0)
    lane = lax.broadcasted_iota(jnp.int32, (GRID_W, LANES), 1)
    upper = lane >= GRID_W
    kc = jnp.where(upper, lane - GRID_W, lane)
    win_start = jnp.clip(qc - WIN_W // 2, 0, GRID_W - WIN_W)
    col_ok = (kc >= win_start) & (kc < win_start + WIN_W)
    idx = jnp.clip(kc - qc + WIN_W - 1, 0, 2 * WIN_W - 2) + jnp.where(upper, BIAS_HI, 0)
    masked = jnp.full((GRID_W, LANES), NEG_INF, F32)
    for dy in range(n_pairs):
        src = jnp.broadcast_to(rp_ref[dy:dy + 1, :], (GRID_W, LANES))
        pt_ref[dy] = jnp.where(col_ok, jnp.take_along_axis(src, idx, axis=1), NEG_INF)
    for kind in range(N_TAB_KINDS):
        for qi in range(Q_ROWS):
            for kp in range(Q_ROWS // 2):
                ki = Q_ROWS * TAB_KEY_BLOCK[kind] + 2 * kp
                ok = (_row_ok(kind, qi, ki), _row_ok(kind, qi, ki + 1))
                tile = pt_ref[ki - qi + WIN_H // 2 - 1] if any(ok) else masked
                if ok == (True, False):
                    tile = jnp.where(upper, NEG_INF, tile)
                elif ok == (False, True):
                    tile = jnp.where(upper, tile, NEG_INF)
                o_ref[kind, qi * GRID_W:(qi + 1) * GRID_W, kp * LANES:(kp + 1) * LANES] = tile


def _bias_tables(rpb):
    n_dx = 2 * WIN_W - 1
    assert n_dx <= BIAS_HI and BIAS_HI + n_dx <= LANES
    pad = lambda a, n: jnp.pad(a, ((0, 0), (0, 0), (0, 2), (0, n - a.shape[-1])))
    rp = jnp.concatenate([pad(rpb[:, :, :-1], BIAS_HI), pad(rpb[:, :, 1:], LANES - BIAS_HI)], axis=-1)
    rows = rp.shape[2]
    return pl.pallas_call(
        _bias_kernel,
        grid=(DEPTH, NA_HEADS),
        in_specs=[pl.BlockSpec((None, None, rows, LANES), lambda l, h: (l, h, 0, 0))],
        out_specs=pl.BlockSpec((None, N_TAB_KINDS, None, Q_BLOCK, Q_BLOCK),
                               lambda l, h: (l, 0, h, 0, 0)),
        out_shape=jax.ShapeDtypeStruct((DEPTH, N_TAB_KINDS, NA_HEADS, Q_BLOCK, Q_BLOCK), F32),
        scratch_shapes=[pltpu.VMEM((2 * WIN_H - 2, GRID_W, LANES), F32)],
        compiler_params=_params(2),
        name="bias_tables",
    )(rp)


def _stacked_weights(w_in, sg_ln_g, sg_ln_b, w_s, b_s, w_pa, w_pb, w_o, ln1_g, ln1_b,
                     w_up, conv_w, conv_b, w_down, ln2_g, ln2_b):
    row = lambda a: a.reshape(DEPTH, 1, -1)
    bf16 = lambda a: a.astype(BF16)
    bs = jnp.repeat(b_s.reshape(DEPTH, SG_GROUPS // 2, 2, CHUNK).transpose(0, 1, 3, 2),
                    SG_WIDTH // SG_GROUPS, axis=-1)
    ws = bf16(w_s).reshape(DEPTH, SG_GROUPS // 2, 2 * CHUNK, CHUNK)
    inproj = (bf16(w_in), row(sg_ln_g), row(sg_ln_b))
    mix = (ws, bs, bf16(w_pa), bf16(w_pb), bf16(w_o), row(ln1_g), row(ln1_b))
    ffn = (bf16(w_up), conv_w, row(conv_b), bf16(w_down), row(ln2_g), row(ln2_b))
    return inproj, mix, ffn


def kernel(x, c, ctx, c_ctx, w_ada, b_ada, w_in, rpb, sg_ln_g, sg_ln_b, w_s, b_s, w_pa, w_pb, w_o,
           ln1_g, ln1_b, w_up, conv_w, conv_b, w_down, ln2_g, ln2_b):
    assert x.shape == (BATCH, SEQ, D_MODEL) and ctx.shape == (BATCH, CTX_LEN, D_MODEL)
    cc = jnp.concatenate([c, c_ctx[None, :], jnp.zeros((MOD_ROWS - BATCH - 1, D_MODEL), F32)], axis=0)
    mods = _modulation(cc, w_ada, b_ada)
    rope = _rope_tables()
    tabs = _bias_tables(rpb)
    (w_in_b, lng, lnb), mix_w, ffn_w = _stacked_weights(
        w_in, sg_ln_g, sg_ln_b, w_s, b_s, w_pa, w_pb, w_o, ln1_g, ln1_b,
        w_up, conv_w, conv_b, w_down, ln2_g, ln2_b)
    xl = x.reshape(BATCH * SEQ, D_MODEL)
    xc = ctx.reshape(BATCH * CTX_LEN, D_MODEL)
    ctx_tiles = dict(tm=2 * CTX_LEN, seq=2 * CTX_LEN)
    for i in range(DEPTH):
        if i < DEPTH - 1:
            c_acts = _in_projection(xc, mods, w_in_b, lng, lnb, None, layer=i, mode="ctx",
                                    **ctx_tiles)
            ctx_kv = (c_acts[1], c_acts[2])
        else:
            ctx_kv = _in_projection(xc, mods, w_in_b, None, None, None, layer=i, mode="ctx_kv",
                                    **ctx_tiles)
        acts = _in_projection(xl, mods, w_in_b, lng, lnb, rope, layer=i, mode="latent", tm=1024,
                              seq=SEQ)
        xl = _mix(xl, mods, acts, ctx_kv, tabs, mix_w, layer=i, latent=True, n_sub=2)
        xl = _conv_ffn(xl, mods, ffn_w, layer=i, latent=True, tm=512, seq=SEQ)
        if i < DEPTH - 1:
            xc = _mix(xc, mods, c_acts, None, None, mix_w, layer=i, latent=False, n_sub=1)
            xc = _conv_ffn(xc, mods, ffn_w, layer=i, latent=False, tm=2 * CTX_LEN, seq=CTX_LEN)
    return xl.reshape(BATCH, SEQ, D_MODEL)
```
